```python
import jax, jax.numpy as jnp
from jax import lax
import numpy as np

D_MODEL = 1024
BATCH = 8
SEQ = 2048
DEPTH = 2

BRANCH_W = D_MODEL // 2
N_BRANCH = 3
N_FOURIER_GROUPS = 4
FOURIER_GW = BRANCH_W // N_FOURIER_GROUPS
CONF_K = 31
SHORT_K = 3
EPS = 1e-6

SPLIT_SIZES = (
    BRANCH_W, BRANCH_W,
    BRANCH_W, BRANCH_W, BRANCH_W,
    BRANCH_W, BRANCH_W, BRANCH_W, BRANCH_W,
    N_BRANCH * D_MODEL,
)
IN_W = sum(SPLIT_SIZES)
SPLIT_IDX = tuple(int(v) for v in np.cumsum(SPLIT_SIZES)[:-1])

kernel_name = "hybrid_fourier_conformer_shortconv_encoder"


def rms_norm(x, g):
    xf = x.astype(jnp.float32)
    y = xf * lax.rsqrt(jnp.mean(xf * xf, axis=-1, keepdims=True) + EPS)
    return (y * g.astype(jnp.float32)).astype(x.dtype)


def layer_norm(x, g, b):
    xf = x.astype(jnp.float32)
    mu = jnp.mean(xf, axis=-1, keepdims=True)
    var = jnp.mean(jnp.square(xf - mu), axis=-1, keepdims=True)
    y = (xf - mu) * lax.rsqrt(var + EPS)
    return (y * g.astype(jnp.float32) + b.astype(jnp.float32)).astype(x.dtype)


def depthwise_conv(u, w, b):
    k, c = w.shape
    pad = (k - 1) // 2
    out = lax.conv_general_dilated(
        u, w[:, None, :].astype(u.dtype), window_strides=(1,), padding=[(pad, pad)],
        dimension_numbers=("NWC", "WIO", "NWC"), feature_group_count=c)
    return out + b


def fourier_mix(u):
    bsz, s, w = u.shape
    ug = u.reshape(bsz, s, N_FOURIER_GROUPS, FOURIER_GW).astype(jnp.float32)
    f = jnp.fft.fftn(ug, axes=(1, 3), norm="ortho").real
    return f.reshape(bsz, s, w).astype(u.dtype)


def setup_inputs(seed: int = 0) -> dict:
    key = jax.random.key(seed)
    ks = jax.random.split(key, 16)
    f32 = jnp.float32
    x = jax.random.normal(ks[0], (BATCH, SEQ, D_MODEL), f32)
    norm_g = 1.0 + 0.05 * jax.random.normal(ks[1], (DEPTH, D_MODEL), f32)
    w_in = jax.random.normal(ks[2], (DEPTH, D_MODEL, IN_W), f32) * D_MODEL ** -0.5
    b_in = 0.01 * jax.random.normal(ks[3], (DEPTH, IN_W), f32)
    conv_c_w = jax.random.normal(ks[4], (DEPTH, CONF_K, BRANCH_W), f32) * CONF_K ** -0.5
    conv_c_b = 0.01 * jax.random.normal(ks[5], (DEPTH, BRANCH_W), f32)
    ln_c_g = 1.0 + 0.05 * jax.random.normal(ks[6], (DEPTH, BRANCH_W), f32)
    ln_c_b = 0.01 * jax.random.normal(ks[7], (DEPTH, BRANCH_W), f32)
    conv_s_w = jax.random.normal(ks[8], (DEPTH, SHORT_K, BRANCH_W), f32) * SHORT_K ** -0.5
    conv_s_b = 0.01 * jax.random.normal(ks[9], (DEPTH, BRANCH_W), f32)
    w_branch = jax.random.normal(ks[10], (DEPTH, N_BRANCH, BRANCH_W, D_MODEL), f32) * BRANCH_W ** -0.5
    w_out = jax.random.normal(ks[11], (DEPTH, D_MODEL, D_MODEL), f32) * D_MODEL ** -0.5
    final_g = 1.0 + 0.05 * jax.random.normal(ks[12], (D_MODEL,), f32)
    return {"x": x, "norm_g": norm_g, "w_in": w_in, "b_in": b_in,
            "conv_c_w": conv_c_w, "conv_c_b": conv_c_b, "ln_c_g": ln_c_g, "ln_c_b": ln_c_b,
            "conv_s_w": conv_s_w, "conv_s_b": conv_s_b, "w_branch": w_branch,
            "w_out": w_out, "final_g": final_g}


def hybrid_layer(x, norm_g, w_in, b_in, conv_c_w, conv_c_b, ln_c_g, ln_c_b,
                 conv_s_w, conv_s_b, w_branch, w_out):
    bsz, s, d = x.shape
    h = rms_norm(x, norm_g)
    p = jnp.einsum("bsd,de->bse", h, w_in) + b_in
    (f_x, f_z, c_a, c_b, c_z, s_bg, s_cg, s_h, s_z, gates) = jnp.split(p, SPLIT_IDX, axis=-1)

    y_f = fourier_mix(f_x) * jax.nn.silu(f_z)

    c = c_a * jax.nn.sigmoid(c_b)
    c = depthwise_conv(c, conv_c_w, conv_c_b)
    c = jax.nn.silu(layer_norm(c, ln_c_g, ln_c_b))
    y_c = c * jax.nn.silu(c_z)

    sc = s_bg * depthwise_conv(s_cg * s_h, conv_s_w, conv_s_b)
    y_s = sc * jax.nn.silu(s_z)

    y = jnp.stack([y_f, y_c, y_s], axis=2)
    yb = jnp.einsum("bskw,kwd->bskd", y, w_branch)
    g = jax.nn.sigmoid(gates.reshape(bsz, s, N_BRANCH, d))
    m = jnp.sum(g * yb, axis=2)
    return x + jnp.einsum("bsd,de->bse", m, w_out)


def reference(x, norm_g, w_in, b_in, conv_c_w, conv_c_b, ln_c_g, ln_c_b,
              conv_s_w, conv_s_b, w_branch, w_out, final_g):
    for l in range(DEPTH):
        x = hybrid_layer(x, norm_g[l], w_in[l], b_in[l], conv_c_w[l], conv_c_b[l],
                         ln_c_g[l], ln_c_b[l], conv_s_w[l], conv_s_b[l],
                         w_branch[l], w_out[l])
    return rms_norm(x, final_g)
```

```python
import functools

import numpy as np
import jax
import jax.numpy as jnp
from jax import lax
from jax.experimental import pallas as pl
from jax.experimental.pallas import tpu as pltpu

D_MODEL = 1024
BATCH = 8
SEQ = 2048
DEPTH = 2
BRANCH_W = D_MODEL // 2
N_BRANCH = 3
N_GROUPS = 4
GROUP_W = BRANCH_W // N_GROUPS
CONF_K = 31
SHORT_K = 3
EPS = 1e-6

TM = 512
NT = SEQ // TM
CONV_ROWS = 64
LANES = 128
CONF_PAD = 16
SHORT_PAD = 8
VMEM_LIMIT = 60 * 1024 * 1024

_FX, _FZ, _CA, _CB, _CZ, _SB, _SC, _SH, _SZ = [(i * BRANCH_W, (i + 1) * BRANCH_W) for i in range(9)]
_GATES = (9 * BRANCH_W, 9 * BRANCH_W + N_BRANCH * D_MODEL)
A_W = 5 * BRANCH_W
B_W = 4 * BRANCH_W + N_BRANCH * D_MODEL

bf16 = jnp.bfloat16
f32 = jnp.float32


def _dft_tables():
    n = np.arange(SEQ, dtype=np.int64)
    ang = 2.0 * np.pi * ((n[:, None] * n[None, :]) % SEQ).astype(np.float64) / SEQ
    cs, ss = np.cos(ang), np.sin(ang)
    c = np.arange(GROUP_W, dtype=np.int64)
    angg = 2.0 * np.pi * ((c[:, None] * c[None, :]) % GROUP_W).astype(np.float64) / GROUP_W
    scale = 1.0 / np.sqrt(float(SEQ * GROUP_W))
    cg = np.kron(np.eye(N_GROUPS), np.cos(angg)) * scale
    sg = np.kron(np.eye(N_GROUPS), np.sin(angg)) * scale
    return (cs.astype(np.float32), ss.astype(np.float32),
            np.concatenate([cg, sg], axis=1).astype(np.float32))


_CS_TABLE, _SS_TABLE, _CSG_TABLE = _dft_tables()


def _dot(a, b):
    return jnp.dot(a, b, preferred_element_type=f32)


def _silu(v):
    return v * jax.nn.sigmoid(v)


def _rms_bf16(x, g):
    y = x * lax.rsqrt(jnp.mean(x * x, axis=-1, keepdims=True) + EPS)
    return (y * g).astype(bf16)


def _kernel_a(x_ref, g_ref, wa_ref, ba_ref, csg_ref, xx_ref, c_ref, u_ref):
    h = _rms_bf16(x_ref[0], g_ref[...])

    def proj(i):
        lo, hi = i * BRANCH_W, (i + 1) * BRANCH_W
        return _dot(h, wa_ref[:, lo:hi]) + ba_ref[:, lo:hi]

    fx = proj(0)
    xx_ref[0] = _dot(fx.astype(bf16), csg_ref[...]).astype(bf16)
    c_ref[0] = proj(1) * jax.nn.sigmoid(proj(2))
    u_ref[0] = proj(3) * proj(4)


def _kernel_f(cs_ref, ss_ref, xx_ref, y_ref):
    y_ref[0] = (_dot(cs_ref[...], xx_ref[0, :, :BRANCH_W])
                - _dot(ss_ref[...], xx_ref[0, :, BRANCH_W:]))


def _fill_window(win_ref, src_ref, pad, j):
    base = pl.multiple_of(j * TM, TM)
    win_ref[pad:pad + TM, :] = src_ref[0, pl.ds(base, TM), :]
    top = src_ref[0, pl.ds(pl.multiple_of(jnp.maximum(base - pad, 0), pad), pad), :]
    win_ref[0:pad, :] = jnp.where(j > 0, top, 0.0)
    bot = src_ref[0, pl.ds(pl.multiple_of(jnp.minimum(base + TM, SEQ - pad), pad), pad), :]
    win_ref[pad + TM:, :] = jnp.where(j < NT - 1, bot, 0.0)


def _depthwise(win_ref, chunk_ref, out_ref, w_ref, b_ref, taps, pad):
    off = pad - (taps - 1) // 2
    rows = CONV_ROWS + 2 * pad

    def body(i, carry):
        r0 = pl.multiple_of(i * CONV_ROWS, CONV_ROWS)
        chunk_ref[0:rows, :] = win_ref[pl.ds(r0, rows), :]
        for lc in range(BRANCH_W // LANES):
            ls = slice(lc * LANES, (lc + 1) * LANES)
            acc = jnp.broadcast_to(b_ref[:, ls], (CONV_ROWS, LANES))
            for k in range(taps):
                acc = acc + w_ref[k:k + 1, ls] * chunk_ref[off + k:off + k + CONV_ROWS, ls]
            out_ref[pl.ds(r0, CONV_ROWS), ls] = acc
        return carry

    lax.fori_loop(0, TM // CONV_ROWS, body, 0)


def _kernel_b(x_ref, g_ref, y_ref, c_ref, u_ref, wb_ref, bb_ref,
              ccw_ref, ccb_ref, lng_ref, lnb_ref, csw_ref, csb_ref,
              wbr_ref, wo_ref, fg_ref, o_ref,
              winc_ref, winu_ref, chunk_ref, conv_ref, *, final):
    j = pl.program_id(1)
    x = x_ref[0]
    h = _rms_bf16(x, g_ref[...])

    def proj(lo, width=BRANCH_W):
        return _dot(h, wb_ref[:, lo:lo + width]) + bb_ref[:, lo:lo + width]

    y_f = y_ref[0] * _silu(proj(0))
    m = jax.nn.sigmoid(proj(4 * BRANCH_W, D_MODEL)) * _dot(y_f.astype(bf16), wbr_ref[0])

    _fill_window(winc_ref, c_ref, CONF_PAD, j)
    _depthwise(winc_ref, chunk_ref, conv_ref, ccw_ref, ccb_ref, CONF_K, CONF_PAD)
    cv = conv_ref[...]
    mu = jnp.mean(cv, axis=-1, keepdims=True)
    cc = cv - mu
    var = jnp.mean(cc * cc, axis=-1, keepdims=True)
    ln = cc * lax.rsqrt(var + EPS) * lng_ref[...] + lnb_ref[...]
    y_c = _silu(ln) * _silu(proj(BRANCH_W))
    m = m + (jax.nn.sigmoid(proj(4 * BRANCH_W + D_MODEL, D_MODEL))
             * _dot(y_c.astype(bf16), wbr_ref[1]))

    _fill_window(winu_ref, u_ref, SHORT_PAD, j)
    _depthwise(winu_ref, chunk_ref, conv_ref, csw_ref, csb_ref, SHORT_K, SHORT_PAD)
    y_s = proj(2 * BRANCH_W) * conv_ref[...] * _silu(proj(3 * BRANCH_W))
    m = m + (jax.nn.sigmoid(proj(4 * BRANCH_W + 2 * D_MODEL, D_MODEL))
             * _dot(y_s.astype(bf16), wbr_ref[2]))

    out = x + _dot(m.astype(bf16), wo_ref[...])
    if final:
        out = out * lax.rsqrt(jnp.mean(out * out, axis=-1, keepdims=True) + EPS) * fg_ref[...]
    o_ref[0] = out


def _const_spec(shape):
    zeros = (0,) * len(shape)
    return pl.BlockSpec(shape, lambda b, j: zeros, pipeline_mode=pl.Buffered(1))


def _params():
    return pltpu.CompilerParams(dimension_semantics=("arbitrary", "arbitrary"),
                                vmem_limit_bytes=VMEM_LIMIT)


def _call_a(x, g, wa, ba, csg):
    tile = lambda w: pl.BlockSpec((1, TM, w), lambda b, j: (b, j, 0))
    return pl.pallas_call(
        _kernel_a,
        grid=(BATCH, NT),
        in_specs=[tile(D_MODEL), _const_spec((1, D_MODEL)), _const_spec((D_MODEL, A_W)),
                  _const_spec((1, A_W)), _const_spec((BRANCH_W, 2 * BRANCH_W))],
        out_specs=[tile(2 * BRANCH_W), tile(BRANCH_W), tile(BRANCH_W)],
        out_shape=[jax.ShapeDtypeStruct((BATCH, SEQ, 2 * BRANCH_W), bf16),
                   jax.ShapeDtypeStruct((BATCH, SEQ, BRANCH_W), f32),
                   jax.ShapeDtypeStruct((BATCH, SEQ, BRANCH_W), f32)],
        compiler_params=_params(),
        name="proj_mix_inputs",
    )(x, g, wa, ba, csg)


def _call_f(cs, ss, xx):
    return pl.pallas_call(
        _kernel_f,
        grid=(BATCH, NT),
        in_specs=[pl.BlockSpec((TM, SEQ), lambda b, j: (j, 0)),
                  pl.BlockSpec((TM, SEQ), lambda b, j: (j, 0)),
                  pl.BlockSpec((1, SEQ, 2 * BRANCH_W), lambda b, j: (b, 0, 0))],
        out_specs=pl.BlockSpec((1, TM, BRANCH_W), lambda b, j: (b, j, 0)),
        out_shape=jax.ShapeDtypeStruct((BATCH, SEQ, BRANCH_W), f32),
        compiler_params=_params(),
        name="seq_dft",
    )(cs, ss, xx)


def _call_b(x, g, y, c, u, wb, bb, ccw, ccb, lng, lnb, csw, csb, wbr, wo, fg, final):
    tile = lambda w: pl.BlockSpec((1, TM, w), lambda b, j: (b, j, 0))
    whole = pl.BlockSpec((1, SEQ, BRANCH_W), lambda b, j: (b, 0, 0))
    return pl.pallas_call(
        functools.partial(_kernel_b, final=final),
        grid=(BATCH, NT),
        in_specs=[tile(D_MODEL), _const_spec((1, D_MODEL)), tile(BRANCH_W), whole, whole,
                  _const_spec((D_MODEL, B_W)), _const_spec((1, B_W)),
                  _const_spec((CONF_K, BRANCH_W)), _const_spec((1, BRANCH_W)),
                  _const_spec((1, BRANCH_W)), _const_spec((1, BRANCH_W)),
                  _const_spec((SHORT_K, BRANCH_W)), _const_spec((1, BRANCH_W)),
                  _const_spec((N_BRANCH, BRANCH_W, D_MODEL)), _const_spec((D_MODEL, D_MODEL)),
                  _const_spec((1, D_MODEL))],
        out_specs=tile(D_MODEL),
        out_shape=jax.ShapeDtypeStruct((BATCH, SEQ, D_MODEL), f32),
        scratch_shapes=[pltpu.VMEM((TM + 2 * CONF_PAD, BRANCH_W), f32),
                        pltpu.VMEM((TM + 2 * SHORT_PAD, BRANCH_W), f32),
                        pltpu.VMEM((CONV_ROWS + 2 * CONF_PAD, BRANCH_W), f32),
                        pltpu.VMEM((TM, BRANCH_W), f32)],
        compiler_params=_params(),
        name="mix_merge_out",
    )(x, g, y, c, u, wb, bb, ccw, ccb, lng, lnb, csw, csb, wbr, wo, fg)


def _cols(w, ranges):
    return jnp.concatenate([w[..., lo:hi] for lo, hi in ranges], axis=-1)


def kernel(x, norm_g, w_in, b_in, conv_c_w, conv_c_b, ln_c_g, ln_c_b,
           conv_s_w, conv_s_b, w_branch, w_out, final_g):
    a_cols = (_FX, _CA, _CB, _SC, _SH)
    b_cols = (_FZ, _CZ, _SB, _SZ, _GATES)
    cs = jnp.asarray(_CS_TABLE).astype(bf16)
    ss = jnp.asarray(_SS_TABLE).astype(bf16)
    csg = jnp.asarray(_CSG_TABLE).astype(bf16)
    row = lambda v: v.reshape(1, -1)
    for l in range(DEPTH):
        g = row(norm_g[l])
        xx, c, u = _call_a(x, g, _cols(w_in[l], a_cols).astype(bf16),
                           row(_cols(b_in[l], a_cols)), csg)
        y = _call_f(cs, ss, xx)
        x = _call_b(x, g, y, c, u,
                    _cols(w_in[l], b_cols).astype(bf16), row(_cols(b_in[l], b_cols)),
                    conv_c_w[l], row(conv_c_b[l]), row(ln_c_g[l]), row(ln_c_b[l]),
                    conv_s_w[l], row(conv_s_b[l]),
                    w_branch[l].astype(bf16), w_out[l].astype(bf16), row(final_g),
                    final=(l == DEPTH - 1))
    return x
```

```python
import functools

import numpy as np
import jax
import jax.numpy as jnp
from jax import lax
from jax.experimental import pallas as pl
from jax.experimental.pallas import tpu as pltpu

D_MODEL = 1024
BATCH = 8
SEQ = 2048
DEPTH = 2
BRANCH_W = D_MODEL // 2
N_BRANCH = 3
N_GROUPS = 4
GROUP_W = BRANCH_W // N_GROUPS
CONF_K = 31
SHORT_K = 3
EPS = 1e-6

TM = 512
NT = SEQ // TM
LANES = 128
N_SLAB = BRANCH_W // LANES
CONV_ROWS = 128
CONF_PAD = 16
SHORT_PAD = 8
VMEM_LIMIT = 60 * 1024 * 1024

_FX, _FZ, _CA, _CB, _CZ, _SB, _SC, _SH, _SZ = [(i * BRANCH_W, (i + 1) * BRANCH_W) for i in range(9)]
_GATES = (9 * BRANCH_W, 9 * BRANCH_W + N_BRANCH * D_MODEL)
A_W = 5 * BRANCH_W
B_W = 4 * BRANCH_W + N_BRANCH * D_MODEL

bf16 = jnp.bfloat16
f32 = jnp.float32


def _dft_tables():
    n = np.arange(SEQ, dtype=np.int64)
    ang = 2.0 * np.pi * ((n[:, None] * n[None, :]) % SEQ).astype(np.float64) / SEQ
    cs, ss = np.cos(ang), np.sin(ang)
    c = np.arange(GROUP_W, dtype=np.int64)
    angg = 2.0 * np.pi * ((c[:, None] * c[None, :]) % GROUP_W).astype(np.float64) / GROUP_W
    scale = 1.0 / np.sqrt(float(SEQ * GROUP_W))
    cg = np.kron(np.eye(N_GROUPS), np.cos(angg)) * scale
    sg = np.kron(np.eye(N_GROUPS), np.sin(angg)) * scale
    return (cs.astype(np.float32), ss.astype(np.float32),
            np.concatenate([cg, sg], axis=1).astype(np.float32))


_CS_TABLE, _SS_TABLE, _CSG_TABLE = _dft_tables()


def _dot(a, b):
    return jnp.dot(a, b, preferred_element_type=f32)


def _silu(v):
    return v * jax.nn.sigmoid(v)


def _rms_bf16(x, g):
    y = x * lax.rsqrt(jnp.mean(x * x, axis=-1, keepdims=True) + EPS)
    return (y * g).astype(bf16)


def _store_slabs(ref, v):
    for s in range(N_SLAB):
        ref[0, s] = v[:, s * LANES:(s + 1) * LANES]


def _kernel_a(x_ref, g_ref, wa_ref, ba_ref, csg_ref, xx_ref, c_ref, u_ref):
    h = _rms_bf16(x_ref[0], g_ref[...])

    def proj(i):
        lo, hi = i * BRANCH_W, (i + 1) * BRANCH_W
        return _dot(h, wa_ref[:, lo:hi]) + ba_ref[:, lo:hi]

    fx = proj(0)
    xx_ref[0] = _dot(fx.astype(bf16), csg_ref[...]).astype(bf16)
    _store_slabs(c_ref, proj(1) * jax.nn.sigmoid(proj(2)))
    _store_slabs(u_ref, proj(3) * proj(4))


def _fill_window(win_ref, src_ref, pad, j):
    base = pl.multiple_of(j * TM, TM)
    top_at = pl.multiple_of(jnp.maximum(base - pad, 0), pad)
    bot_at = pl.multiple_of(jnp.minimum(base + TM, SEQ - pad), pad)
    for s in range(N_SLAB):
        win_ref[s, pad:pad + TM, :] = src_ref[0, s, pl.ds(base, TM), :]
        win_ref[s, 0:pad, :] = jnp.where(j > 0, src_ref[0, s, pl.ds(top_at, pad), :], 0.0)
        win_ref[s, pad + TM:, :] = jnp.where(j < NT - 1, src_ref[0, s, pl.ds(bot_at, pad), :], 0.0)


def _depthwise(win_ref, out_ref, w_ref, b_ref, taps, pad):
    off = pad - (taps - 1) // 2
    for r0 in range(0, TM, CONV_ROWS):
        for s in range(N_SLAB):
            ls = slice(s * LANES, (s + 1) * LANES)
            acc = jnp.broadcast_to(b_ref[:, ls], (CONV_ROWS, LANES))
            for k in range(taps):
                acc = acc + w_ref[k:k + 1, ls] * win_ref[s, r0 + off + k:r0 + off + k + CONV_ROWS, :]
            out_ref[r0:r0 + CONV_ROWS, ls] = acc


def _kernel_f(cs_ref, ss_ref, xx_ref, c_ref, u_ref,
              ccw_ref, ccb_ref, lng_ref, lnb_ref, csw_ref, csb_ref,
              y_ref, yc_ref, ys_ref, winc_ref, winu_ref):
    j = pl.program_id(1)
    y_ref[0] = (_dot(cs_ref[...], xx_ref[0, :, :BRANCH_W])
                - _dot(ss_ref[...], xx_ref[0, :, BRANCH_W:]))

    _fill_window(winc_ref, c_ref, CONF_PAD, j)
    _depthwise(winc_ref, yc_ref.at[0], ccw_ref, ccb_ref, CONF_K, CONF_PAD)
    cv = yc_ref[0]
    mu = jnp.mean(cv, axis=-1, keepdims=True)
    cc = cv - mu
    var = jnp.mean(cc * cc, axis=-1, keepdims=True)
    yc_ref[0] = _silu(cc * lax.rsqrt(var + EPS) * lng_ref[...] + lnb_ref[...])

    _fill_window(winu_ref, u_ref, SHORT_PAD, j)
    _depthwise(winu_ref, ys_ref.at[0], csw_ref, csb_ref, SHORT_K, SHORT_PAD)


def _kernel_b(x_ref, g_ref, y_ref, yc_ref, ys_ref, wb_ref, bb_ref,
              wbr_ref, wo_ref, fg_ref, o_ref, *, final):
    x = x_ref[0]
    h = _rms_bf16(x, g_ref[...])

    def proj(lo, width=BRANCH_W):
        return _dot(h, wb_ref[:, lo:lo + width]) + bb_ref[:, lo:lo + width]

    def gate(k):
        return jax.nn.sigmoid(proj(4 * BRANCH_W + k * D_MODEL, D_MODEL))

    y_f = y_ref[0] * _silu(proj(0))
    m = gate(0) * _dot(y_f.astype(bf16), wbr_ref[0])
    y_c = yc_ref[0] * _silu(proj(BRANCH_W))
    m = m + gate(1) * _dot(y_c.astype(bf16), wbr_ref[1])
    y_s = proj(2 * BRANCH_W) * ys_ref[0] * _silu(proj(3 * BRANCH_W))
    m = m + gate(2) * _dot(y_s.astype(bf16), wbr_ref[2])

    out = x + _dot(m.astype(bf16), wo_ref[...])
    if final:
        out = out * lax.rsqrt(jnp.mean(out * out, axis=-1, keepdims=True) + EPS) * fg_ref[...]
    o_ref[0] = out


def _const_spec(shape):
    zeros = (0,) * len(shape)
    return pl.BlockSpec(shape, lambda b, j: zeros, pipeline_mode=pl.Buffered(1))


def _params():
    return pltpu.CompilerParams(dimension_semantics=("arbitrary", "arbitrary"),
                                vmem_limit_bytes=VMEM_LIMIT)


def _call_a(x, g, wa, ba, csg):
    tile = lambda w: pl.BlockSpec((1, TM, w), lambda b, j: (b, j, 0))
    slab_tile = pl.BlockSpec((1, N_SLAB, TM, LANES), lambda b, j: (b, 0, j, 0))
    slab_shape = jax.ShapeDtypeStruct((BATCH, N_SLAB, SEQ, LANES), f32)
    return pl.pallas_call(
        _kernel_a,
        grid=(BATCH, NT),
        in_specs=[tile(D_MODEL), _const_spec((1, D_MODEL)), _const_spec((D_MODEL, A_W)),
                  _const_spec((1, A_W)), _const_spec((BRANCH_W, 2 * BRANCH_W))],
        out_specs=[tile(2 * BRANCH_W), slab_tile, slab_tile],
        out_shape=[jax.ShapeDtypeStruct((BATCH, SEQ, 2 * BRANCH_W), bf16), slab_shape, slab_shape],
        compiler_params=_params(),
        name="proj_mix_inputs",
    )(x, g, wa, ba, csg)


def _call_f(cs, ss, xx, c, u, ccw, ccb, lng, lnb, csw, csb):
    rows = pl.BlockSpec((TM, SEQ), lambda b, j: (j, 0))
    whole = pl.BlockSpec((1, N_SLAB, SEQ, LANES), lambda b, j: (b, 0, 0, 0))
    tile = pl.BlockSpec((1, TM, BRANCH_W), lambda b, j: (b, j, 0))
    shape = jax.ShapeDtypeStruct((BATCH, SEQ, BRANCH_W), f32)
    return pl.pallas_call(
        _kernel_f,
        grid=(BATCH, NT),
        in_specs=[rows, rows,
                  pl.BlockSpec((1, SEQ, 2 * BRANCH_W), lambda b, j: (b, 0, 0)), whole, whole,
                  _const_spec((CONF_K, BRANCH_W)), _const_spec((1, BRANCH_W)),
                  _const_spec((1, BRANCH_W)), _const_spec((1, BRANCH_W)),
                  _const_spec((SHORT_K, BRANCH_W)), _const_spec((1, BRANCH_W))],
        out_specs=[tile, tile, tile],
        out_shape=[shape, shape, shape],
        scratch_shapes=[pltpu.VMEM((N_SLAB, TM + 2 * CONF_PAD, LANES), f32),
                        pltpu.VMEM((N_SLAB, TM + 2 * SHORT_PAD, LANES), f32)],
        compiler_params=_params(),
        name="seq_mix",
    )(cs, ss, xx, c, u, ccw, ccb, lng, lnb, csw, csb)


def _call_b(x, g, y, yc, ys, wb, bb, wbr, wo, fg, final):
    tile = lambda w: pl.BlockSpec((1, TM, w), lambda b, j: (b, j, 0))
    return pl.pallas_call(
        functools.partial(_kernel_b, final=final),
        grid=(BATCH, NT),
        in_specs=[tile(D_MODEL), _const_spec((1, D_MODEL)),
                  tile(BRANCH_W), tile(BRANCH_W), tile(BRANCH_W),
                  _const_spec((D_MODEL, B_W)), _const_spec((1, B_W)),
                  _const_spec((N_BRANCH, BRANCH_W, D_MODEL)), _const_spec((D_MODEL, D_MODEL)),
                  _const_spec((1, D_MODEL))],
        out_specs=tile(D_MODEL),
        out_shape=jax.ShapeDtypeStruct((BATCH, SEQ, D_MODEL), f32),
        compiler_params=_params(),
        name="merge_out",
    )(x, g, y, yc, ys, wb, bb, wbr, wo, fg)


def _cols(w, ranges):
    return jnp.concatenate([w[..., lo:hi] for lo, hi in ranges], axis=-1)


def kernel(x, norm_g, w_in, b_in, conv_c_w, conv_c_b, ln_c_g, ln_c_b,
           conv_s_w, conv_s_b, w_branch, w_out, final_g):
    a_cols = (_FX, _CA, _CB, _SC, _SH)
    b_cols = (_FZ, _CZ, _SB, _SZ, _GATES)
    cs = jnp.asarray(_CS_TABLE).astype(bf16)
    ss = jnp.asarray(_SS_TABLE).astype(bf16)
    csg = jnp.asarray(_CSG_TABLE).astype(bf16)
    row = lambda v: v.reshape(1, -1)
    for l in range(DEPTH):
        g = row(norm_g[l])
        xx, c, u = _call_a(x, g, _cols(w_in[l], a_cols).astype(bf16),
                           row(_cols(b_in[l], a_cols)), csg)
        y, yc, ys = _call_f(cs, ss, xx, c, u,
                            conv_c_w[l], row(conv_c_b[l]), row(ln_c_g[l]), row(ln_c_b[l]),
                            conv_s_w[l], row(conv_s_b[l]))
        x = _call_b(x, g, y, yc, ys,
                    _cols(w_in[l], b_cols).astype(bf16), row(_cols(b_in[l], b_cols)),
                    w_branch[l].astype(bf16), w_out[l].astype(bf16), row(final_g),
                    final=(l == DEPTH - 1))
    return x
```

```python
import functools

import numpy as np
import jax
import jax.numpy as jnp
from jax import lax
from jax.experimental import pallas as pl
from jax.experimental.pallas import tpu as pltpu

D_MODEL = 1024
BATCH = 8
SEQ = 2048
DEPTH = 2
BRANCH_W = D_MODEL // 2
N_BRANCH = 3
N_GROUPS = 4
GROUP_W = BRANCH_W // N_GROUPS
CONF_K = 31
SHORT_K = 3
EPS = 1e-6
IN_W = 9 * BRANCH_W + N_BRANCH * D_MODEL

TM = 512
NT = SEQ // TM
LANES = 128
N_SLAB = BRANCH_W // LANES
DFT_PLANES = 2
CONV_ROWS = 64
CONV_PER_STEP = TM // CONV_ROWS // DFT_PLANES
CONF_PAD = 16
SHORT_PAD = 8
VMEM_LIMIT = 60 * 1024 * 1024

_FX, _FZ, _CA, _CB, _CZ, _SB, _SC, _SH, _SZ = range(9)
_GATE0 = 9
A_BLOCKS = (_FX, _CA, _CB, _SC, _SH)
B_BLOCKS = (_FZ, _CZ, _SB, _SZ) + tuple(range(_GATE0, _GATE0 + 2 * N_BRANCH))

bf16 = jnp.bfloat16
f32 = jnp.float32


def _dft_tables():
    n = np.arange(SEQ, dtype=np.int64)
    ang = 2.0 * np.pi * ((n[:, None] * n[None, :]) % SEQ).astype(np.float64) / SEQ
    seq = np.stack([np.cos(ang), -np.sin(ang)])
    c = np.arange(GROUP_W, dtype=np.int64)
    angg = 2.0 * np.pi * ((c[:, None] * c[None, :]) % GROUP_W).astype(np.float64) / GROUP_W
    scale = 1.0 / np.sqrt(float(SEQ * GROUP_W))
    cg = np.kron(np.eye(N_GROUPS), np.cos(angg)) * scale
    sg = np.kron(np.eye(N_GROUPS), np.sin(angg)) * scale
    return seq.astype(np.float32), np.concatenate([cg, sg], axis=1).astype(np.float32)


_SEQ_TABLE, _CSG_TABLE = _dft_tables()


def _dot(a, b):
    return jnp.dot(a, b, preferred_element_type=f32)


def _silu(v):
    return v * jax.nn.sigmoid(v)


def _rms_bf16(x, g):
    y = x * lax.rsqrt(jnp.mean(x * x, axis=-1, keepdims=True) + EPS)
    return (y * g).astype(bf16)


def _bias(b_ref, block):
    return b_ref[0, :, block * BRANCH_W:(block + 1) * BRANCH_W]


def _store_slabs(ref, v):
    for s in range(N_SLAB):
        ref[0, s] = v[:, s * LANES:(s + 1) * LANES]


def _kernel_a(x_ref, g_ref, w_fx, w_ca, w_cb, w_sc, w_sh, b_ref, csg_ref, xx_ref, c_ref, u_ref):
    h = _rms_bf16(x_ref[0], g_ref[0])

    def proj(w_ref, block):
        return _dot(h, w_ref[0]) + _bias(b_ref, block)

    xx = _dot(proj(w_fx, _FX).astype(bf16), csg_ref[...]).astype(bf16)
    xx_ref[0, 0] = xx[:, :BRANCH_W]
    xx_ref[0, 1] = xx[:, BRANCH_W:]
    _store_slabs(c_ref, proj(w_ca, _CA) * jax.nn.sigmoid(proj(w_cb, _CB)))
    _store_slabs(u_ref, proj(w_sc, _SC) * proj(w_sh, _SH))


def _fill_window(win_ref, src_ref, pad, j):
    base = pl.multiple_of(j * TM, TM)
    top_at = pl.multiple_of(jnp.maximum(base - pad, 0), pad)
    bot_at = pl.multiple_of(jnp.minimum(base + TM, SEQ - pad), pad)
    for s in range(N_SLAB):
        win_ref[s, pad:pad + TM, :] = src_ref[0, s, pl.ds(base, TM), :]
        win_ref[s, 0:pad, :] = jnp.where(j > 0, src_ref[0, s, pl.ds(top_at, pad), :], 0.0)
        win_ref[s, pad + TM:2 * pad + TM, :] = jnp.where(
            j < NT - 1, src_ref[0, s, pl.ds(bot_at, pad), :], 0.0)


def _depthwise(win_ref, r0, w_ref, b_ref, taps, pad):
    off = pad - (taps - 1) // 2
    park = TM + 2 * pad
    for s in range(N_SLAB):
        ls = slice(s * LANES, (s + 1) * LANES)
        acc = jnp.broadcast_to(b_ref[0, :, ls], (CONV_ROWS, LANES))
        for k in range(taps):
            acc = acc + w_ref[0, k:k + 1, ls] * win_ref[s, pl.ds(r0 + off + k, CONV_ROWS), :]
        win_ref[s, park:park + CONV_ROWS, :] = acc
    return jnp.concatenate([win_ref[s, park:park + CONV_ROWS, :] for s in range(N_SLAB)], axis=1)


def _kernel_f(tab_ref, xx_ref, c_ref, u_ref, ccw_ref, ccb_ref, csw_ref, csb_ref,
              y_ref, cv_ref, ys_ref, winc_ref, winu_ref):
    j = pl.program_id(1)
    _fill_window(winc_ref, c_ref, CONF_PAD, j)
    _fill_window(winu_ref, u_ref, SHORT_PAD, j)
    y_ref[0] = jnp.zeros((TM, BRANCH_W), f32)

    def body(i, carry):
        y_ref[0] += _dot(tab_ref[i], xx_ref[0, i])
        for q in range(CONV_PER_STEP):
            r0 = pl.multiple_of((i * CONV_PER_STEP + q) * CONV_ROWS, CONV_ROWS)
            cv_ref[0, pl.ds(r0, CONV_ROWS), :] = _depthwise(
                winc_ref, r0, ccw_ref, ccb_ref, CONF_K, CONF_PAD)
            ys_ref[0, pl.ds(r0, CONV_ROWS), :] = _depthwise(
                winu_ref, r0, csw_ref, csb_ref, SHORT_K, SHORT_PAD)
        return carry

    lax.fori_loop(0, DFT_PLANES, body, 0)


def _kernel_b(x_ref, g_ref, y_ref, cv_ref, ys_ref,
              w_fz, w_cz, w_sb, w_sz, w_g0a, w_g0b, w_g1a, w_g1b, w_g2a, w_g2b,
              b_ref, lng_ref, lnb_ref, wbr_ref, wo_ref, fg_ref, o_ref, *, final):
    x = x_ref[0]
    h = _rms_bf16(x, g_ref[0])

    def proj(w_ref, block):
        return _dot(h, w_ref[0]) + _bias(b_ref, block)

    def gate(k, w_lo, w_hi):
        blk = _GATE0 + 2 * k
        return jax.nn.sigmoid(jnp.concatenate([proj(w_lo, blk), proj(w_hi, blk + 1)], axis=1))

    y_f = y_ref[0] * _silu(proj(w_fz, _FZ))
    m = gate(0, w_g0a, w_g0b) * _dot(y_f.astype(bf16), wbr_ref[0, 0])

    cv = cv_ref[0]
    cc = cv - jnp.mean(cv, axis=-1, keepdims=True)
    var = jnp.mean(cc * cc, axis=-1, keepdims=True)
    ln = cc * lax.rsqrt(var + EPS) * lng_ref[0] + lnb_ref[0]
    y_c = _silu(ln) * _silu(proj(w_cz, _CZ))
    m = m + gate(1, w_g1a, w_g1b) * _dot(y_c.astype(bf16), wbr_ref[0, 1])

    y_s = proj(w_sb, _SB) * ys_ref[0] * _silu(proj(w_sz, _SZ))
    m = m + gate(2, w_g2a, w_g2b) * _dot(y_s.astype(bf16), wbr_ref[0, 2])

    out = x + _dot(m.astype(bf16), wo_ref[0])
    if final:
        out = out * lax.rsqrt(jnp.mean(out * out, axis=-1, keepdims=True) + EPS) * fg_ref[...]
    o_ref[0] = out


def _resident(shape, *index):
    return pl.BlockSpec(shape, lambda b, j: index, pipeline_mode=pl.Buffered(1))


def _layer_row(width, l):
    return _resident((1, 1, width), l, 0, 0)


def _w_in_block(l, block):
    return _resident((1, D_MODEL, BRANCH_W), l, 0, block)


def _params():
    return pltpu.CompilerParams(dimension_semantics=("arbitrary", "arbitrary"),
                                vmem_limit_bytes=VMEM_LIMIT)


def _tile(width):
    return pl.BlockSpec((1, TM, width), lambda b, j: (b, j, 0))


_SLAB_WHOLE = pl.BlockSpec((1, N_SLAB, SEQ, LANES), lambda b, j: (b, 0, 0, 0))
_SLAB_TILE = pl.BlockSpec((1, N_SLAB, TM, LANES), lambda b, j: (b, 0, j, 0))
_SLAB_SHAPE = jax.ShapeDtypeStruct((BATCH, N_SLAB, SEQ, LANES), f32)
_BRANCH_SHAPE = jax.ShapeDtypeStruct((BATCH, SEQ, BRANCH_W), f32)


def _call_a(l, x, norm_g, w_in, b_in, csg):
    return pl.pallas_call(
        _kernel_a,
        grid=(BATCH, NT),
        in_specs=[_tile(D_MODEL), _layer_row(D_MODEL, l)]
                 + [_w_in_block(l, blk) for blk in A_BLOCKS]
                 + [_layer_row(IN_W, l), _resident((BRANCH_W, 2 * BRANCH_W), 0, 0)],
        out_specs=[pl.BlockSpec((1, 2, TM, BRANCH_W), lambda b, j: (b, 0, j, 0)),
                   _SLAB_TILE, _SLAB_TILE],
        out_shape=[jax.ShapeDtypeStruct((BATCH, 2, SEQ, BRANCH_W), bf16), _SLAB_SHAPE, _SLAB_SHAPE],
        compiler_params=_params(),
        name="proj_mix_inputs",
    )(x, norm_g, *([w_in] * len(A_BLOCKS)), b_in, csg)


def _call_f(l, tab, xx, c, u, ccw, ccb, csw, csb):
    return pl.pallas_call(
        _kernel_f,
        grid=(BATCH, NT),
        in_specs=[pl.BlockSpec((DFT_PLANES, TM, SEQ), lambda b, j: (0, j, 0)),
                  pl.BlockSpec((1, 2, SEQ, BRANCH_W), lambda b, j: (b, 0, 0, 0)),
                  _SLAB_WHOLE, _SLAB_WHOLE,
                  _resident((1, CONF_K, BRANCH_W), l, 0, 0), _layer_row(BRANCH_W, l),
                  _resident((1, SHORT_K, BRANCH_W), l, 0, 0), _layer_row(BRANCH_W, l)],
        out_specs=[_tile(BRANCH_W)] * 3,
        out_shape=[_BRANCH_SHAPE] * 3,
        scratch_shapes=[pltpu.VMEM((N_SLAB, TM + 2 * CONF_PAD + CONV_ROWS, LANES), f32),
                        pltpu.VMEM((N_SLAB, TM + 2 * SHORT_PAD + CONV_ROWS, LANES), f32)],
        compiler_params=_params(),
        name="seq_mix",
    )(tab, xx, c, u, ccw, ccb, csw, csb)


def _call_b(l, x, norm_g, y, cv, ys, w_in, b_in, lng, lnb, wbr, wo, fg):
    return pl.pallas_call(
        functools.partial(_kernel_b, final=(l == DEPTH - 1)),
        grid=(BATCH, NT),
        in_specs=[_tile(D_MODEL), _layer_row(D_MODEL, l),
                  _tile(BRANCH_W), _tile(BRANCH_W), _tile(BRANCH_W)]
                 + [_w_in_block(l, blk) for blk in B_BLOCKS]
                 + [_layer_row(IN_W, l), _layer_row(BRANCH_W, l), _layer_row(BRANCH_W, l),
                    _resident((1, N_BRANCH, BRANCH_W, D_MODEL), l, 0, 0, 0),
                    _resident((1, D_MODEL, D_MODEL), l, 0, 0),
                    _resident((1, D_MODEL), 0, 0)],
        out_specs=_tile(D_MODEL),
        out_shape=jax.ShapeDtypeStruct((BATCH, SEQ, D_MODEL), f32),
        compiler_params=_params(),
        name="merge_out",
    )(x, norm_g, y, cv, ys, *([w_in] * len(B_BLOCKS)), b_in, lng, lnb, wbr, wo, fg)


def kernel(x, norm_g, w_in, b_in, conv_c_w, conv_c_b, ln_c_g, ln_c_b,
           conv_s_w, conv_s_b, w_branch, w_out, final_g):
    tab = jnp.asarray(_SEQ_TABLE).astype(bf16)
    csg = jnp.asarray(_CSG_TABLE).astype(bf16)
    w_in, w_branch, w_out = w_in.astype(bf16), w_branch.astype(bf16), w_out.astype(bf16)
    rows = lambda v: v.reshape(DEPTH, 1, -1)
    norm_g, b_in, conv_c_b, ln_c_g, ln_c_b, conv_s_b = map(
        rows, (norm_g, b_in, conv_c_b, ln_c_g, ln_c_b, conv_s_b))
    final_g = final_g.reshape(1, -1)
    for l in range(DEPTH):
        xx, c, u = _call_a(l, x, norm_g, w_in, b_in, csg)
        y, cv, ys = _call_f(l, tab, xx, c, u, conv_c_w, conv_c_b, conv_s_w, conv_s_b)
        x = _call_b(l, x, norm_g, y, cv, ys, w_in, b_in, ln_c_g, ln_c_b, w_branch, w_out, final_g)
    return x
```

```python
import functools

import numpy as np
import jax
import jax.numpy as jnp
from jax import lax
from jax.experimental import pallas as pl
from jax.experimental.pallas import tpu as pltpu

D_MODEL = 1024
BATCH = 8
SEQ = 2048
DEPTH = 2
BRANCH_W = D_MODEL // 2
N_BRANCH = 3
N_GROUPS = 4
GROUP_W = BRANCH_W // N_GROUPS
CONF_K = 31
SHORT_K = 3
EPS = 1e-6
IN_W = 9 * BRANCH_W + N_BRANCH * D_MODEL

TM = 512
NT = SEQ // TM
TM_MERGE = 512
LANES = 128
N_SLAB = BRANCH_W // LANES
DFT_PLANES = 2
CONV_ROWS = 64
CONV_PER_STEP = TM // CONV_ROWS // DFT_PLANES
SLABS_IN_FLIGHT = 1
CONF_PAD = 16
SHORT_PAD = 8
VMEM_LIMIT = 60 * 1024 * 1024

_FX, _FZ, _CA, _CB, _CZ, _SB, _SC, _SH, _SZ = range(9)
_GATE0 = 9
A_BLOCKS = (_FX, _CA, _CB, _SC, _SH)
B_BLOCKS = (_FZ, _CZ, _SB, _SZ) + tuple(range(_GATE0, _GATE0 + 2 * N_BRANCH))

bf16 = jnp.bfloat16
f32 = jnp.float32


def _dft_tables():
    n = np.arange(SEQ, dtype=np.int64)
    ang = 2.0 * np.pi * ((n[:, None] * n[None, :]) % SEQ).astype(np.float64) / SEQ
    seq = np.stack([np.cos(ang), -np.sin(ang)])
    c = np.arange(GROUP_W, dtype=np.int64)
    angg = 2.0 * np.pi * ((c[:, None] * c[None, :]) % GROUP_W).astype(np.float64) / GROUP_W
    scale = 1.0 / np.sqrt(float(SEQ * GROUP_W))
    cg = np.kron(np.eye(N_GROUPS), np.cos(angg)) * scale
    sg = np.kron(np.eye(N_GROUPS), np.sin(angg)) * scale
    return seq.astype(np.float32), np.concatenate([cg, sg], axis=1).astype(np.float32)


_SEQ_TABLE, _CSG_TABLE = _dft_tables()


def _dot(a, b):
    return jnp.dot(a, b, preferred_element_type=f32)


def _silu(v):
    return v * jax.nn.sigmoid(v)


def _rms_bf16(x, g):
    y = x * lax.rsqrt(jnp.mean(x * x, axis=-1, keepdims=True) + EPS)
    return (y * g).astype(bf16)


def _bias(b_ref, block):
    return b_ref[0, :, block * BRANCH_W:(block + 1) * BRANCH_W]


def _store_slabs(ref, v):
    for s in range(N_SLAB):
        ref[0, s] = v[:, s * LANES:(s + 1) * LANES]


def _kernel_a(x_ref, g_ref, w_fx, w_ca, w_cb, w_sc, w_sh, b_ref, csg_ref, xx_ref, c_ref, u_ref):
    h = _rms_bf16(x_ref[0], g_ref[0])

    def proj(w_ref, block):
        return _dot(h, w_ref[0]) + _bias(b_ref, block)

    xx = _dot(proj(w_fx, _FX).astype(bf16), csg_ref[...]).astype(bf16)
    xx_ref[0, 0] = xx[:, :BRANCH_W]
    xx_ref[0, 1] = xx[:, BRANCH_W:]
    _store_slabs(c_ref, proj(w_ca, _CA) * jax.nn.sigmoid(proj(w_cb, _CB)))
    _store_slabs(u_ref, proj(w_sc, _SC) * proj(w_sh, _SH))


def _fill_window(win_ref, src_ref, pad, j):
    base = pl.multiple_of(j * TM, TM)
    top_at = pl.multiple_of(jnp.maximum(base - pad, 0), pad)
    bot_at = pl.multiple_of(jnp.minimum(base + TM, SEQ - pad), pad)
    for s in range(N_SLAB):
        win_ref[s, pad:pad + TM, :] = src_ref[0, s, pl.ds(base, TM), :]
        win_ref[s, 0:pad, :] = jnp.where(j > 0, src_ref[0, s, pl.ds(top_at, pad), :], 0.0)
        win_ref[s, pad + TM:2 * pad + TM, :] = jnp.where(
            j < NT - 1, src_ref[0, s, pl.ds(bot_at, pad), :], 0.0)


def _depthwise(win_ref, r0, w_ref, b_ref, taps, pad):
    off = pad - (taps - 1) // 2
    park = TM + 2 * pad
    accs = []
    for s in range(N_SLAB):
        ls = slice(s * LANES, (s + 1) * LANES)
        acc = jnp.broadcast_to(b_ref[0, :, ls], (CONV_ROWS, LANES))
        for k in range(taps):
            acc = acc + w_ref[0, k:k + 1, ls] * win_ref[s, pl.ds(r0 + off + k, CONV_ROWS), :]
        accs.append(acc)
        if s % SLABS_IN_FLIGHT == SLABS_IN_FLIGHT - 1:
            for t in range(s - SLABS_IN_FLIGHT + 1, s + 1):
                win_ref[t, park:park + CONV_ROWS, :] = accs[t]
    return jnp.concatenate([win_ref[s, park:park + CONV_ROWS, :] for s in range(N_SLAB)], axis=1)


def _kernel_f(tab_ref, xx_ref, c_ref, u_ref, ccw_ref, ccb_ref, csw_ref, csb_ref,
              y_ref, cv_ref, ys_ref, winc_ref, winu_ref):
    j = pl.program_id(1)
    _fill_window(winc_ref, c_ref, CONF_PAD, j)
    _fill_window(winu_ref, u_ref, SHORT_PAD, j)
    y_ref[0] = jnp.zeros((TM, BRANCH_W), f32)

    def body(i, carry):
        y_ref[0] += _dot(tab_ref[i, pl.ds(pl.multiple_of(j * TM, TM), TM), :], xx_ref[0, i])
        for q in range(CONV_PER_STEP):
            r0 = pl.multiple_of((i * CONV_PER_STEP + q) * CONV_ROWS, CONV_ROWS)
            cv_ref[0, pl.ds(r0, CONV_ROWS), :] = _depthwise(
                winc_ref, r0, ccw_ref, ccb_ref, CONF_K, CONF_PAD)
            ys_ref[0, pl.ds(r0, CONV_ROWS), :] = _depthwise(
                winu_ref, r0, csw_ref, csb_ref, SHORT_K, SHORT_PAD)
        return carry

    lax.fori_loop(0, DFT_PLANES, body, 0)


def _kernel_b(x_ref, g_ref, y_ref, cv_ref, ys_ref,
              w_fz, w_cz, w_sb, w_sz, w_g0a, w_g0b, w_g1a, w_g1b, w_g2a, w_g2b,
              b_ref, lng_ref, lnb_ref, wbr_ref, wo_ref, fg_ref, o_ref, *, final):
    x = x_ref[0]
    h = _rms_bf16(x, g_ref[0])

    def proj(w_ref, block):
        return _dot(h, w_ref[0]) + _bias(b_ref, block)

    def gate(k, w_lo, w_hi):
        blk = _GATE0 + 2 * k
        return jax.nn.sigmoid(jnp.concatenate([proj(w_lo, blk), proj(w_hi, blk + 1)], axis=1))

    y_f = y_ref[0] * _silu(proj(w_fz, _FZ))
    m = gate(0, w_g0a, w_g0b) * _dot(y_f.astype(bf16), wbr_ref[0, 0])

    cv = cv_ref[0]
    cc = cv - jnp.mean(cv, axis=-1, keepdims=True)
    var = jnp.mean(cc * cc, axis=-1, keepdims=True)
    ln = cc * lax.rsqrt(var + EPS) * lng_ref[0] + lnb_ref[0]
    y_c = _silu(ln) * _silu(proj(w_cz, _CZ))
    m = m + gate(1, w_g1a, w_g1b) * _dot(y_c.astype(bf16), wbr_ref[0, 1])

    y_s = proj(w_sb, _SB) * ys_ref[0] * _silu(proj(w_sz, _SZ))
    m = m + gate(2, w_g2a, w_g2b) * _dot(y_s.astype(bf16), wbr_ref[0, 2])

    out = x + _dot(m.astype(bf16), wo_ref[0])
    if final:
        out = out * lax.rsqrt(jnp.mean(out * out, axis=-1, keepdims=True) + EPS) * fg_ref[...]
    o_ref[0] = out


def _resident(shape, *index):
    return pl.BlockSpec(shape, lambda b, j: index, pipeline_mode=pl.Buffered(1))


def _layer_row(width, l):
    return _resident((1, 1, width), l, 0, 0)


def _w_in_block(l, block):
    return _resident((1, D_MODEL, BRANCH_W), l, 0, block)


def _params():
    return pltpu.CompilerParams(dimension_semantics=("arbitrary", "arbitrary"),
                                vmem_limit_bytes=VMEM_LIMIT)


def _tile(width, rows=TM):
    return pl.BlockSpec((1, rows, width), lambda b, j: (b, j, 0))


_SLAB_WHOLE = pl.BlockSpec((1, N_SLAB, SEQ, LANES), lambda b, j: (b, 0, 0, 0))
_SLAB_TILE = pl.BlockSpec((1, N_SLAB, TM, LANES), lambda b, j: (b, 0, j, 0))
_SLAB_SHAPE = jax.ShapeDtypeStruct((BATCH, N_SLAB, SEQ, LANES), f32)
_BRANCH_SHAPE = jax.ShapeDtypeStruct((BATCH, SEQ, BRANCH_W), f32)


def _call_a(l, x, norm_g, w_in, b_in, csg):
    return pl.pallas_call(
        _kernel_a,
        grid=(BATCH, NT),
        in_specs=[_tile(D_MODEL), _layer_row(D_MODEL, l)]
                 + [_w_in_block(l, blk) for blk in A_BLOCKS]
                 + [_layer_row(IN_W, l), _resident((BRANCH_W, 2 * BRANCH_W), 0, 0)],
        out_specs=[pl.BlockSpec((1, 2, TM, BRANCH_W), lambda b, j: (b, 0, j, 0)),
                   _SLAB_TILE, _SLAB_TILE],
        out_shape=[jax.ShapeDtypeStruct((BATCH, 2, SEQ, BRANCH_W), bf16), _SLAB_SHAPE, _SLAB_SHAPE],
        compiler_params=_params(),
        name="proj_mix_inputs",
    )(x, norm_g, *([w_in] * len(A_BLOCKS)), b_in, csg)


def _call_f(l, tab, xx, c, u, ccw, ccb, csw, csb):
    return pl.pallas_call(
        _kernel_f,
        grid=(BATCH, NT),
        in_specs=[_resident((DFT_PLANES, SEQ, SEQ), 0, 0, 0),
                  pl.BlockSpec((1, 2, SEQ, BRANCH_W), lambda b, j: (b, 0, 0, 0)),
                  _SLAB_WHOLE, _SLAB_WHOLE,
                  _resident((1, CONF_K, BRANCH_W), l, 0, 0), _layer_row(BRANCH_W, l),
                  _resident((1, SHORT_K, BRANCH_W), l, 0, 0), _layer_row(BRANCH_W, l)],
        out_specs=[_tile(BRANCH_W)] * 3,
        out_shape=[_BRANCH_SHAPE] * 3,
        scratch_shapes=[pltpu.VMEM((N_SLAB, TM + 2 * CONF_PAD + CONV_ROWS, LANES), f32),
                        pltpu.VMEM((N_SLAB, TM + 2 * SHORT_PAD + CONV_ROWS, LANES), f32)],
        compiler_params=_params(),
        name="seq_mix",
    )(tab, xx, c, u, ccw, ccb, csw, csb)


def _call_b(l, x, norm_g, y, cv, ys, w_in, b_in, lng, lnb, wbr, wo, fg):
    return pl.pallas_call(
        functools.partial(_kernel_b, final=(l == DEPTH - 1)),
        grid=(BATCH, SEQ // TM_MERGE),
        in_specs=[_tile(D_MODEL, TM_MERGE), _layer_row(D_MODEL, l)]
                 + [_tile(BRANCH_W, TM_MERGE)] * 3
                 + [_w_in_block(l, blk) for blk in B_BLOCKS]
                 + [_layer_row(IN_W, l), _layer_row(BRANCH_W, l), _layer_row(BRANCH_W, l),
                    _resident((1, N_BRANCH, BRANCH_W, D_MODEL), l, 0, 0, 0),
                    _resident((1, D_MODEL, D_MODEL), l, 0, 0),
                    _resident((1, D_MODEL), 0, 0)],
        out_specs=_tile(D_MODEL, TM_MERGE),
        out_shape=jax.ShapeDtypeStruct((BATCH, SEQ, D_MODEL), f32),
        compiler_params=_params(),
        name="merge_out",
    )(x, norm_g, y, cv, ys, *([w_in] * len(B_BLOCKS)), b_in, lng, lnb, wbr, wo, fg)


def kernel(x, norm_g, w_in, b_in, conv_c_w, conv_c_b, ln_c_g, ln_c_b,
           conv_s_w, conv_s_b, w_branch, w_out, final_g):
    tab = jnp.asarray(_SEQ_TABLE).astype(bf16)
    csg = jnp.asarray(_CSG_TABLE).astype(bf16)
    w_in, w_branch, w_out = w_in.astype(bf16), w_branch.astype(bf16), w_out.astype(bf16)
    rows = lambda v: v.reshape(DEPTH, 1, -1)
    norm_g, b_in, conv_c_b, ln_c_g, ln_c_b, conv_s_b = map(
        rows, (norm_g, b_in, conv_c_b, ln_c_g, ln_c_b, conv_s_b))
    final_g = final_g.reshape(1, -1)
    for l in range(DEPTH):
        xx, c, u = _call_a(l, x, norm_g, w_in, b_in, csg)
        y, cv, ys = _call_f(l, tab, xx, c, u, conv_c_w, conv_c_b, conv_s_w, conv_s_b)
        x = _call_b(l, x, norm_g, y, cv, ys, w_in, b_in, ln_c_g, ln_c_b, w_branch, w_out, final_g)
    return x
```

```python
import functools

import numpy as np
import jax
import jax.numpy as jnp
from jax import lax
from jax.experimental import pallas as pl
from jax.experimental.pallas import tpu as pltpu

D_MODEL = 1024
BATCH = 8
SEQ = 2048
DEPTH = 2
BRANCH_W = D_MODEL // 2
N_BRANCH = 3
N_GROUPS = 4
GROUP_W = BRANCH_W // N_GROUPS
PAIR_W = 2 * GROUP_W
CONF_K = 31
SHORT_K = 3
EPS = 1e-6
IN_W = 9 * BRANCH_W + N_BRANCH * D_MODEL

TM = 512
NT = SEQ // TM
TM_MERGE = 1024
LANES = 128
N_SLAB = BRANCH_W // LANES
DFT_PLANES = 2
CONV_ROWS = 64
CONV_PER_STEP = TM // CONV_ROWS // DFT_PLANES
CONF_PAD = 16
SHORT_PAD = 8
VMEM_LIMIT = 60 * 1024 * 1024

_FX, _FZ, _CA, _CB, _CZ, _SB, _SC, _SH, _SZ = range(9)
_GATE0 = 9
A_BLOCKS = (_FX, _CA, _CB, _SC, _SH)
B_BLOCKS = (_FZ, _CZ, _SB, _SZ) + tuple(range(_GATE0, _GATE0 + 2 * N_BRANCH))

bf16 = jnp.bfloat16
f32 = jnp.float32


def _dft_tables():
    n = np.arange(SEQ, dtype=np.int64)
    ang = 2.0 * np.pi * ((n[:, None] * n[None, :]) % SEQ).astype(np.float64) / SEQ
    seq = np.stack([np.cos(ang), -np.sin(ang)])
    c = np.arange(GROUP_W, dtype=np.int64)
    angg = 2.0 * np.pi * ((c[:, None] * c[None, :]) % GROUP_W).astype(np.float64) / GROUP_W
    scale = 1.0 / np.sqrt(float(SEQ * GROUP_W))
    pair = np.eye(PAIR_W // GROUP_W)
    cg = np.kron(pair, np.cos(angg)) * scale
    sg = np.kron(pair, np.sin(angg)) * scale
    return seq.astype(np.float32), np.concatenate([cg, sg], axis=1).astype(np.float32)


_SEQ_TABLE, _CSG_TABLE = _dft_tables()


def _dot(a, b):
    return jnp.dot(a, b, preferred_element_type=f32)


def _silu(v):
    return v * jax.nn.sigmoid(v)


def _rms_bf16(x, g):
    y = x * lax.rsqrt(jnp.mean(x * x, axis=-1, keepdims=True) + EPS)
    return (y * g).astype(bf16)


def _bias(b_ref, block):
    return b_ref[0, :, block * BRANCH_W:(block + 1) * BRANCH_W]


def _store_slabs(ref, v):
    for s in range(N_SLAB):
        ref[0, s] = v[:, s * LANES:(s + 1) * LANES]


def _kernel_a(x_ref, g_ref, w_fx, w_ca, w_cb, w_sc, w_sh, b_ref, csg_ref, xx_ref, c_ref, u_ref):
    h = _rms_bf16(x_ref[0], g_ref[0])

    def proj(w_ref, block):
        return _dot(h, w_ref[0]) + _bias(b_ref, block)

    fx = proj(w_fx, _FX).astype(bf16)
    for p in range(BRANCH_W // PAIR_W):
        cols = slice(p * PAIR_W, (p + 1) * PAIR_W)
        xx = _dot(fx[:, cols], csg_ref[...]).astype(bf16)
        xx_ref[0, 0, :, cols] = xx[:, :PAIR_W]
        xx_ref[0, 1, :, cols] = xx[:, PAIR_W:]
    _store_slabs(c_ref, proj(w_ca, _CA) * jax.nn.sigmoid(proj(w_cb, _CB)))
    _store_slabs(u_ref, proj(w_sc, _SC) * proj(w_sh, _SH))


def _fill_window(win_ref, top_ref, mid_ref, bot_ref, pad, j):
    for s in range(N_SLAB):
        win_ref[s, 0:pad, :] = jnp.where(j > 0, top_ref[0, s], 0.0)
        win_ref[s, pad:pad + TM, :] = mid_ref[0, s]
        win_ref[s, pad + TM:2 * pad + TM, :] = jnp.where(j < NT - 1, bot_ref[0, s], 0.0)


def _depthwise(win_ref, r0, rows, w_ref, b_ref, taps, pad, park_at=None):
    off = pad - (taps - 1) // 2
    accs = []
    for s in range(N_SLAB):
        ls = slice(s * LANES, (s + 1) * LANES)
        acc = jnp.broadcast_to(b_ref[0, :, ls], (rows, LANES))
        for k in range(taps):
            acc = acc + w_ref[0, k:k + 1, ls] * win_ref[s, pl.ds(r0 + off + k, rows), :]
        if park_at is not None:
            win_ref[s, park_at:park_at + rows, :] = acc
            acc = win_ref[s, park_at:park_at + rows, :]
        accs.append(acc)
    return jnp.concatenate(accs, axis=1)


def _kernel_f(tab_ref, xx_ref, c_top, c_mid, c_bot, u_top, u_mid, u_bot,
              ccw_ref, ccb_ref, csw_ref, csb_ref,
              y_ref, cv_ref, ys_ref, winc_ref, winu_ref):
    j = pl.program_id(1)
    _fill_window(winc_ref, c_top, c_mid, c_bot, CONF_PAD, j)
    _fill_window(winu_ref, u_top, u_mid, u_bot, SHORT_PAD, j)
    y_ref[0] = jnp.zeros((TM, BRANCH_W), f32)

    def body(i, carry):
        y_ref[0] += _dot(tab_ref[i, pl.ds(pl.multiple_of(j * TM, TM), TM), :], xx_ref[0, i])
        for q in range(CONV_PER_STEP):
            r0 = pl.multiple_of((i * CONV_PER_STEP + q) * CONV_ROWS, CONV_ROWS)
            cv_ref[0, pl.ds(r0, CONV_ROWS), :] = _depthwise(
                winc_ref, r0, CONV_ROWS, ccw_ref, ccb_ref, CONF_K, CONF_PAD,
                park_at=TM + 2 * CONF_PAD)
            ys_ref[0, pl.ds(r0, CONV_ROWS), :] = _depthwise(
                winu_ref, r0, CONV_ROWS, csw_ref, csb_ref, SHORT_K, SHORT_PAD,
                park_at=TM + 2 * SHORT_PAD)
        return carry

    lax.fori_loop(0, DFT_PLANES, body, 0)


def _kernel_b(x_ref, g_ref, y_ref, cv_ref, ys_ref,
              w_fz, w_cz, w_sb, w_sz, w_g0a, w_g0b, w_g1a, w_g1b, w_g2a, w_g2b,
              b_ref, lng_ref, lnb_ref, wbr_ref, wo_ref, fg_ref, o_ref, *, final):
    x = x_ref[0]
    h = _rms_bf16(x, g_ref[0])

    def proj(w_ref, block):
        return _dot(h, w_ref[0]) + _bias(b_ref, block)

    def gate(k, w_lo, w_hi):
        blk = _GATE0 + 2 * k
        return jax.nn.sigmoid(jnp.concatenate([proj(w_lo, blk), proj(w_hi, blk + 1)], axis=1))

    y_f = y_ref[0] * _silu(proj(w_fz, _FZ))
    m = gate(0, w_g0a, w_g0b) * _dot(y_f.astype(bf16), wbr_ref[0, 0])

    cv = cv_ref[0]
    cc = cv - jnp.mean(cv, axis=-1, keepdims=True)
    var = jnp.mean(cc * cc, axis=-1, keepdims=True)
    ln = cc * lax.rsqrt(var + EPS) * lng_ref[0] + lnb_ref[0]
    y_c = _silu(ln) * _silu(proj(w_cz, _CZ))
    m = m + gate(1, w_g1a, w_g1b) * _dot(y_c.astype(bf16), wbr_ref[0, 1])

    y_s = proj(w_sb, _SB) * ys_ref[0] * _silu(proj(w_sz, _SZ))
    m = m + gate(2, w_g2a, w_g2b) * _dot(y_s.astype(bf16), wbr_ref[0, 2])

    out = x + _dot(m.astype(bf16), wo_ref[0])
    if final:
        out = out * lax.rsqrt(jnp.mean(out * out, axis=-1, keepdims=True) + EPS) * fg_ref[...]
    o_ref[0] = out


def _resident(shape, *index):
    return pl.BlockSpec(shape, lambda b, j: index, pipeline_mode=pl.Buffered(1))


def _layer_row(width, l):
    return _resident((1, 1, width), l, 0, 0)


def _w_in_block(l, block):
    return _resident((1, D_MODEL, BRANCH_W), l, 0, block)


def _params():
    return pltpu.CompilerParams(dimension_semantics=("arbitrary", "arbitrary"),
                                vmem_limit_bytes=VMEM_LIMIT)


def _tile(width, rows=TM):
    return pl.BlockSpec((1, rows, width), lambda b, j: (b, j, 0))


_SLAB_TILE = pl.BlockSpec((1, N_SLAB, TM, LANES), lambda b, j: (b, 0, j, 0))
_SLAB_SHAPE = jax.ShapeDtypeStruct((BATCH, N_SLAB, SEQ, LANES), f32)
_BRANCH_SHAPE = jax.ShapeDtypeStruct((BATCH, SEQ, BRANCH_W), f32)


def _call_a(l, x, norm_g, w_in, b_in, csg):
    return pl.pallas_call(
        _kernel_a,
        grid=(BATCH, NT),
        in_specs=[_tile(D_MODEL), _layer_row(D_MODEL, l)]
                 + [_w_in_block(l, blk) for blk in A_BLOCKS]
                 + [_layer_row(IN_W, l), _resident((PAIR_W, 2 * PAIR_W), 0, 0)],
        out_specs=[pl.BlockSpec((1, 2, TM, BRANCH_W), lambda b, j: (b, 0, j, 0)),
                   _SLAB_TILE, _SLAB_TILE],
        out_shape=[jax.ShapeDtypeStruct((BATCH, 2, SEQ, BRANCH_W), bf16), _SLAB_SHAPE, _SLAB_SHAPE],
        compiler_params=_params(),
        name="proj_mix_inputs",
    )(x, norm_g, *([w_in] * len(A_BLOCKS)), b_in, csg)


def _halo_specs(pad):
    per_tile = TM // pad
    last = SEQ // pad - 1
    shape = (1, N_SLAB, pad, LANES)
    return [pl.BlockSpec(shape, lambda b, j: (b, 0, jnp.maximum(j * per_tile - 1, 0), 0)),
            _SLAB_TILE,
            pl.BlockSpec(shape, lambda b, j: (b, 0, jnp.minimum((j + 1) * per_tile, last), 0))]


def _call_f(l, tab, xx, c, u, ccw, ccb, csw, csb):
    return pl.pallas_call(
        _kernel_f,
        grid=(BATCH, NT),
        in_specs=[_resident((DFT_PLANES, SEQ, SEQ), 0, 0, 0),
                  pl.BlockSpec((1, 2, SEQ, BRANCH_W), lambda b, j: (b, 0, 0, 0))]
                 + _halo_specs(CONF_PAD) + _halo_specs(SHORT_PAD) + [
                  _resident((1, CONF_K, BRANCH_W), l, 0, 0), _layer_row(BRANCH_W, l),
                  _resident((1, SHORT_K, BRANCH_W), l, 0, 0), _layer_row(BRANCH_W, l)],
        out_specs=[_tile(BRANCH_W)] * 3,
        out_shape=[_BRANCH_SHAPE] * 3,
        scratch_shapes=[pltpu.VMEM((N_SLAB, TM + 2 * CONF_PAD + CONV_ROWS, LANES), f32),
                        pltpu.VMEM((N_SLAB, TM + 2 * SHORT_PAD + CONV_ROWS, LANES), f32)],
        compiler_params=_params(),
        name="seq_mix",
    )(tab, xx, c, c, c, u, u, u, ccw, ccb, csw, csb)


def _call_b(l, x, norm_g, y, cv, ys, w_in, b_in, lng, lnb, wbr, wo, fg):
    return pl.pallas_call(
        functools.partial(_kernel_b, final=(l == DEPTH - 1)),
        grid=(BATCH, SEQ // TM_MERGE),
        in_specs=[_tile(D_MODEL, TM_MERGE), _layer_row(D_MODEL, l)]
                 + [_tile(BRANCH_W, TM_MERGE)] * 3
                 + [_w_in_block(l, blk) for blk in B_BLOCKS]
                 + [_layer_row(IN_W, l), _layer_row(BRANCH_W, l), _layer_row(BRANCH_W, l),
                    _resident((1, N_BRANCH, BRANCH_W, D_MODEL), l, 0, 0, 0),
                    _resident((1, D_MODEL, D_MODEL), l, 0, 0),
                    _resident((1, D_MODEL), 0, 0)],
        out_specs=_tile(D_MODEL, TM_MERGE),
        out_shape=jax.ShapeDtypeStruct((BATCH, SEQ, D_MODEL), f32),
        compiler_params=_params(),
        name="merge_out",
    )(x, norm_g, y, cv, ys, *([w_in] * len(B_BLOCKS)), b_in, lng, lnb, wbr, wo, fg)


def kernel(x, norm_g, w_in, b_in, conv_c_w, conv_c_b, ln_c_g, ln_c_b,
           conv_s_w, conv_s_b, w_branch, w_out, final_g):
    tab = jnp.asarray(_SEQ_TABLE).astype(bf16)
    csg = jnp.asarray(_CSG_TABLE).astype(bf16)
    w_in, w_branch, w_out = w_in.astype(bf16), w_branch.astype(bf16), w_out.astype(bf16)
    rows = lambda v: v.reshape(DEPTH, 1, -1)
    norm_g, b_in, conv_c_b, ln_c_g, ln_c_b, conv_s_b = map(
        rows, (norm_g, b_in, conv_c_b, ln_c_g, ln_c_b, conv_s_b))
    final_g = final_g.reshape(1, -1)
    for l in range(DEPTH):
        xx, c, u = _call_a(l, x, norm_g, w_in, b_in, csg)
        y, cv, ys = _call_f(l, tab, xx, c, u, conv_c_w, conv_c_b, conv_s_w, conv_s_b)
        x = _call_b(l, x, norm_g, y, cv, ys, w_in, b_in, ln_c_g, ln_c_b, w_branch, w_out, final_g)
    return x
```

```python
import functools

import numpy as np
import jax
import jax.numpy as jnp
from jax import lax
from jax.experimental import pallas as pl
from jax.experimental.pallas import tpu as pltpu

D_MODEL = 1024
BATCH = 8
SEQ = 2048
DEPTH = 2
BRANCH_W = D_MODEL // 2
N_BRANCH = 3
N_GROUPS = 4
GROUP_W = BRANCH_W // N_GROUPS
PAIR_W = 2 * GROUP_W
CONF_K = 31
SHORT_K = 3
EPS = 1e-6
IN_W = 9 * BRANCH_W + N_BRANCH * D_MODEL

TM = 512
NT = SEQ // TM
TM_MERGE = 1024
LANES = 128
N_SLAB = BRANCH_W // LANES
HALF = SEQ // 2
NT_HALF = NT // 2
CONV_ROWS = 32
CONV_PER_TILE = TM // CONV_ROWS
CONF_PAD = 16
SHORT_PAD = 8
VMEM_LIMIT = 60 * 1024 * 1024

_FX, _FZ, _CA, _CB, _CZ, _SB, _SC, _SH, _SZ = range(9)
_GATE0 = 9
A_BLOCKS = (_FX, _CA, _CB, _SC, _SH)
B_BLOCKS = (_FZ, _CZ, _SB, _SZ) + tuple(range(_GATE0, _GATE0 + 2 * N_BRANCH))

bf16 = jnp.bfloat16
f32 = jnp.float32


def _dft_tables():
    k = np.arange(HALF, dtype=np.int64)[:, None]
    m = np.arange(HALF, dtype=np.int64)[None, :]
    seq = []
    for parity in range(2):
        ang = 2.0 * np.pi * (((2 * m + parity) * k) % SEQ).astype(np.float64) / SEQ
        seq.append(np.concatenate([np.cos(ang), -np.sin(ang)], axis=1))
    seq = np.stack(seq)
    c = np.arange(GROUP_W, dtype=np.int64)
    angg = 2.0 * np.pi * ((c[:, None] * c[None, :]) % GROUP_W).astype(np.float64) / GROUP_W
    scale = 1.0 / np.sqrt(float(SEQ * GROUP_W))
    pair = np.eye(PAIR_W // GROUP_W)
    cg = np.kron(pair, np.cos(angg)) * scale
    sg = np.kron(pair, np.sin(angg)) * scale
    return seq.astype(np.float32), np.concatenate([cg, sg], axis=1).astype(np.float32)


_SEQ_TABLE, _CSG_TABLE = _dft_tables()


def _dot(a, b):
    return jnp.dot(a, b, preferred_element_type=f32)


def _silu(v):
    return v * jax.nn.sigmoid(v)


def _rms_bf16(x, g):
    y = x * lax.rsqrt(jnp.mean(x * x, axis=-1, keepdims=True) + EPS)
    return (y * g).astype(bf16)


def _bias(b_ref, block):
    return b_ref[0, :, block * BRANCH_W:(block + 1) * BRANCH_W]


def _store_slabs(ref, v):
    for s in range(N_SLAB):
        ref[0, s] = v[:, s * LANES:(s + 1) * LANES]


def _kernel_a(x_ref, g_ref, w_fx, w_ca, w_cb, w_sc, w_sh, b_ref, csg_ref,
              xx_ref, c_ref, u_ref, split_ref):
    h = _rms_bf16(x_ref[0], g_ref[0])

    def proj(w_ref, block):
        return _dot(h, w_ref[0]) + _bias(b_ref, block)

    fx = proj(w_fx, _FX).astype(bf16)
    slabs_per_plane = PAIR_W // LANES
    for p in range(BRANCH_W // PAIR_W):
        cols = slice(p * PAIR_W, (p + 1) * PAIR_W)
        xx = _dot(fx[:, cols], csg_ref[...])
        for q in range(2 * slabs_per_plane):
            split_ref[q] = xx[:, q * LANES:(q + 1) * LANES]
        for parity in range(2):
            rows = [split_ref[q, pl.ds(parity, TM // 2, stride=2), :]
                    for q in range(2 * slabs_per_plane)]
            for plane in range(2):
                part = rows[plane * slabs_per_plane:(plane + 1) * slabs_per_plane]
                xx_ref[0, parity, plane, :, cols] = jnp.concatenate(part, axis=1).astype(bf16)
    _store_slabs(c_ref, proj(w_ca, _CA) * jax.nn.sigmoid(proj(w_cb, _CB)))
    _store_slabs(u_ref, proj(w_sc, _SC) * proj(w_sh, _SH))


def _fill_window(win_ref, top_ref, mid_ref, bot_ref, pad, j):
    for s in range(N_SLAB):
        win_ref[s, 0:pad, :] = jnp.where(j > 0, top_ref[0, s], 0.0)
        win_ref[s, pad:pad + TM, :] = mid_ref[0, s]
        win_ref[s, pad + TM:2 * pad + TM, :] = jnp.where(j < NT - 1, bot_ref[0, s], 0.0)


def _depthwise(wins_ref, half, r0, rows, w_ref, b_ref, taps, pad):
    off = pad - (taps - 1) // 2
    park = TM + 2 * pad
    accs = []
    for s in range(N_SLAB):
        ls = slice(s * LANES, (s + 1) * LANES)
        acc = jnp.broadcast_to(b_ref[0, :, ls], (rows, LANES))
        for k in range(taps):
            acc = acc + w_ref[0, k:k + 1, ls] * wins_ref[half, s, pl.ds(r0 + off + k, rows), :]
        wins_ref[0, s, park:park + rows, :] = acc
        accs.append(wins_ref[0, s, park:park + rows, :])
    return jnp.concatenate(accs, axis=1)


def _kernel_f(tab_ref, xx_ref,
              c0_top, c0_mid, c0_bot, c1_top, c1_mid, c1_bot,
              u0_top, u0_mid, u0_bot, u1_top, u1_mid, u1_bot,
              ccw_ref, ccb_ref, csw_ref, csb_ref,
              y_ref, cv_ref, ys_ref, winc_ref, winu_ref, eo_ref):
    j = pl.program_id(1)
    halos = ((c0_top, c0_mid, c0_bot, u0_top, u0_mid, u0_bot),
             (c1_top, c1_mid, c1_bot, u1_top, u1_mid, u1_bot))
    for half, (ct, cm, cb, ut, um, ub) in enumerate(halos):
        tile = j + half * NT_HALF
        _fill_window(winc_ref.at[half], ct, cm, cb, CONF_PAD, tile)
        _fill_window(winu_ref.at[half], ut, um, ub, SHORT_PAD, tile)
    k_rows = pl.ds(pl.multiple_of(j * TM, TM), TM)

    def body(half, carry):
        eo_ref[half] = _dot(tab_ref[half, k_rows, :], xx_ref[0, half])
        for q in range(CONV_PER_TILE):
            rows = slice(q * CONV_ROWS, (q + 1) * CONV_ROWS)
            cv_ref[0, half, rows, :] = _depthwise(
                winc_ref, half, q * CONV_ROWS, CONV_ROWS, ccw_ref, ccb_ref, CONF_K, CONF_PAD)
            ys_ref[0, half, rows, :] = _depthwise(
                winu_ref, half, q * CONV_ROWS, CONV_ROWS, csw_ref, csb_ref, SHORT_K, SHORT_PAD)
        return carry

    lax.fori_loop(0, jnp.minimum(j + 2, 2), body, 0)
    even, odd = eo_ref[0], eo_ref[1]
    y_ref[0, 0] = even + odd
    y_ref[0, 1] = even - odd


def _kernel_b(x_ref, g_ref, y_ref, cv_ref, ys_ref,
              w_fz, w_cz, w_sb, w_sz, w_g0a, w_g0b, w_g1a, w_g1b, w_g2a, w_g2b,
              b_ref, lng_ref, lnb_ref, wbr_ref, wo_ref, fg_ref, o_ref, *, final):
    x = x_ref[0]
    h = _rms_bf16(x, g_ref[0])

    def proj(w_ref, block):
        return _dot(h, w_ref[0]) + _bias(b_ref, block)

    def gate(k, w_lo, w_hi):
        blk = _GATE0 + 2 * k
        return jax.nn.sigmoid(jnp.concatenate([proj(w_lo, blk), proj(w_hi, blk + 1)], axis=1))

    y_f = y_ref[0] * _silu(proj(w_fz, _FZ))
    m = gate(0, w_g0a, w_g0b) * _dot(y_f.astype(bf16), wbr_ref[0, 0])

    cv = cv_ref[0]
    cc = cv - jnp.mean(cv, axis=-1, keepdims=True)
    var = jnp.mean(cc * cc, axis=-1, keepdims=True)
    ln = cc * lax.rsqrt(var + EPS) * lng_ref[0] + lnb_ref[0]
    y_c = _silu(ln) * _silu(proj(w_cz, _CZ))
    m = m + gate(1, w_g1a, w_g1b) * _dot(y_c.astype(bf16), wbr_ref[0, 1])

    y_s = proj(w_sb, _SB) * ys_ref[0] * _silu(proj(w_sz, _SZ))
    m = m + gate(2, w_g2a, w_g2b) * _dot(y_s.astype(bf16), wbr_ref[0, 2])

    out = x + _dot(m.astype(bf16), wo_ref[0])
    if final:
        out = out * lax.rsqrt(jnp.mean(out * out, axis=-1, keepdims=True) + EPS) * fg_ref[...]
    o_ref[0] = out


def _resident(shape, *index):
    return pl.BlockSpec(shape, lambda b, j: index, pipeline_mode=pl.Buffered(1))


def _layer_row(width, l):
    return _resident((1, 1, width), l, 0, 0)


def _w_in_block(l, block):
    return _resident((1, D_MODEL, BRANCH_W), l, 0, block)


def _params():
    return pltpu.CompilerParams(dimension_semantics=("arbitrary", "arbitrary"),
                                vmem_limit_bytes=VMEM_LIMIT)


def _tile(width, rows=TM):
    return pl.BlockSpec((1, rows, width), lambda b, j: (b, j, 0))


_SLAB_TILE = pl.BlockSpec((1, N_SLAB, TM, LANES), lambda b, j: (b, 0, j, 0))
_SLAB_SHAPE = jax.ShapeDtypeStruct((BATCH, N_SLAB, SEQ, LANES), f32)
_BRANCH_SHAPE = jax.ShapeDtypeStruct((BATCH, SEQ, BRANCH_W), f32)


def _call_a(l, x, norm_g, w_in, b_in, csg):
    return pl.pallas_call(
        _kernel_a,
        grid=(BATCH, NT),
        in_specs=[_tile(D_MODEL), _layer_row(D_MODEL, l)]
                 + [_w_in_block(l, blk) for blk in A_BLOCKS]
                 + [_layer_row(IN_W, l), _resident((PAIR_W, 2 * PAIR_W), 0, 0)],
        out_specs=[pl.BlockSpec((1, 2, 2, TM // 2, BRANCH_W), lambda b, j: (b, 0, 0, j, 0)),
                   _SLAB_TILE, _SLAB_TILE],
        out_shape=[jax.ShapeDtypeStruct((BATCH, 2, 2, HALF, BRANCH_W), bf16),
                   _SLAB_SHAPE, _SLAB_SHAPE],
        scratch_shapes=[pltpu.VMEM((2 * PAIR_W // LANES, TM, LANES), f32)],
        compiler_params=_params(),
        name="proj_mix_inputs",
    )(x, norm_g, *([w_in] * len(A_BLOCKS)), b_in, csg)


def _halo_specs(pad, half):
    per_tile = TM // pad
    last = SEQ // pad - 1
    shape = (1, N_SLAB, pad, LANES)
    first_tile = half * NT_HALF
    return [pl.BlockSpec(shape, lambda b, j: (
                b, 0, jnp.maximum((j + first_tile) * per_tile - 1, 0), 0)),
            pl.BlockSpec((1, N_SLAB, TM, LANES), lambda b, j: (b, 0, j + first_tile, 0)),
            pl.BlockSpec(shape, lambda b, j: (
                b, 0, jnp.minimum((j + first_tile + 1) * per_tile, last), 0))]


def _call_f(l, tab, xx, c, u, ccw, ccb, csw, csb):
    halves = pl.BlockSpec((1, 2, TM, BRANCH_W), lambda b, j: (b, 0, j, 0))
    shape = jax.ShapeDtypeStruct((BATCH, 2, HALF, BRANCH_W), f32)
    return pl.pallas_call(
        _kernel_f,
        grid=(BATCH, NT_HALF),
        in_specs=[_resident((2, HALF, SEQ), 0, 0, 0),
                  pl.BlockSpec((1, 2, SEQ, BRANCH_W), lambda b, j: (b, 0, 0, 0))]
                 + _halo_specs(CONF_PAD, 0) + _halo_specs(CONF_PAD, 1)
                 + _halo_specs(SHORT_PAD, 0) + _halo_specs(SHORT_PAD, 1) + [
                  _resident((1, CONF_K, BRANCH_W), l, 0, 0), _layer_row(BRANCH_W, l),
                  _resident((1, SHORT_K, BRANCH_W), l, 0, 0), _layer_row(BRANCH_W, l)],
        out_specs=[halves] * 3,
        out_shape=[shape] * 3,
        scratch_shapes=[pltpu.VMEM((2, N_SLAB, TM + 2 * CONF_PAD + CONV_ROWS, LANES), f32),
                        pltpu.VMEM((2, N_SLAB, TM + 2 * SHORT_PAD + CONV_ROWS, LANES), f32),
                        pltpu.VMEM((2, TM, BRANCH_W), f32)],
        compiler_params=_params(),
        name="seq_mix",
    )(tab, xx, *([c] * 6), *([u] * 6), ccw, ccb, csw, csb)


def _call_b(l, x, norm_g, y, cv, ys, w_in, b_in, lng, lnb, wbr, wo, fg):
    return pl.pallas_call(
        functools.partial(_kernel_b, final=(l == DEPTH - 1)),
        grid=(BATCH, SEQ // TM_MERGE),
        in_specs=[_tile(D_MODEL, TM_MERGE), _layer_row(D_MODEL, l)]
                 + [_tile(BRANCH_W, TM_MERGE)] * 3
                 + [_w_in_block(l, blk) for blk in B_BLOCKS]
                 + [_layer_row(IN_W, l), _layer_row(BRANCH_W, l), _layer_row(BRANCH_W, l),
                    _resident((1, N_BRANCH, BRANCH_W, D_MODEL), l, 0, 0, 0),
                    _resident((1, D_MODEL, D_MODEL), l, 0, 0),
                    _resident((1, D_MODEL), 0, 0)],
        out_specs=_tile(D_MODEL, TM_MERGE),
        out_shape=jax.ShapeDtypeStruct((BATCH, SEQ, D_MODEL), f32),
        compiler_params=_params(),
        name="merge_out",
    )(x, norm_g, y, cv, ys, *([w_in] * len(B_BLOCKS)), b_in, lng, lnb, wbr, wo, fg)


def kernel(x, norm_g, w_in, b_in, conv_c_w, conv_c_b, ln_c_g, ln_c_b,
           conv_s_w, conv_s_b, w_branch, w_out, final_g):
    tab = jnp.asarray(_SEQ_TABLE).astype(bf16)
    csg = jnp.asarray(_CSG_TABLE).astype(bf16)
    w_in, w_branch, w_out = w_in.astype(bf16), w_branch.astype(bf16), w_out.astype(bf16)
    rows = lambda v: v.reshape(DEPTH, 1, -1)
    norm_g, b_in, conv_c_b, ln_c_g, ln_c_b, conv_s_b = map(
        rows, (norm_g, b_in, conv_c_b, ln_c_g, ln_c_b, conv_s_b))
    final_g = final_g.reshape(1, -1)
    for l in range(DEPTH):
        xx, c, u = _call_a(l, x, norm_g, w_in, b_in, csg)
        xx = xx.reshape(BATCH, 2, SEQ, BRANCH_W)
        y, cv, ys = (v.reshape(BATCH, SEQ, BRANCH_W) for v in
                     _call_f(l, tab, xx, c, u, conv_c_w, conv_c_b, conv_s_w, conv_s_b))
        x = _call_b(l, x, norm_g, y, cv, ys, w_in, b_in, ln_c_g, ln_c_b, w_branch, w_out, final_g)
    return x
```

```python
import functools

import numpy as np
import jax
import jax.numpy as jnp
from jax import lax
from jax.experimental import pallas as pl
from jax.experimental.pallas import tpu as pltpu

D_MODEL = 1024
BATCH = 8
SEQ = 2048
DEPTH = 2
BRANCH_W = D_MODEL // 2
N_BRANCH = 3
N_GROUPS = 4
GROUP_W = BRANCH_W // N_GROUPS
PAIR_W = 2 * GROUP_W
CONF_K = 31
SHORT_K = 3
EPS = 1e-6
IN_W = 9 * BRANCH_W + N_BRANCH * D_MODEL

TM = 512
NT = SEQ // TM
LANES = 128
N_SLAB = BRANCH_W // LANES
HALF = SEQ // 2
CONV_ROWS = 32
CONF_PAD = 16
SHORT_PAD = 8
VMEM_LIMIT = 60 * 1024 * 1024

_FX, _FZ, _CA, _CB, _CZ, _SB, _SC, _SH, _SZ = range(9)
_GATE0 = 9
A_BLOCKS = (_FX, _CA, _CB, _SC, _SH)
B_BLOCKS = (_FZ, _CZ, _SB, _SZ) + tuple(range(_GATE0, _GATE0 + 2 * N_BRANCH))

bf16 = jnp.bfloat16
f32 = jnp.float32


def _dft_tables():
    k = np.arange(HALF, dtype=np.int64)[:, None]
    m = np.arange(HALF, dtype=np.int64)[None, :]
    seq = []
    for parity in range(2):
        ang = 2.0 * np.pi * (((2 * m + parity) * k) % SEQ).astype(np.float64) / SEQ
        seq.append(np.concatenate([np.cos(ang), -np.sin(ang)], axis=1))
    seq = np.stack(seq)
    c = np.arange(GROUP_W, dtype=np.int64)
    angg = 2.0 * np.pi * ((c[:, None] * c[None, :]) % GROUP_W).astype(np.float64) / GROUP_W
    scale = 1.0 / np.sqrt(float(SEQ * GROUP_W))
    pair = np.eye(PAIR_W // GROUP_W)
    cg = np.kron(pair, np.cos(angg)) * scale
    sg = np.kron(pair, np.sin(angg)) * scale
    return seq.astype(np.float32), np.concatenate([cg, sg], axis=1).astype(np.float32)


_SEQ_TABLE, _CSG_TABLE = _dft_tables()


def _dot(a, b):
    return jnp.dot(a, b, preferred_element_type=f32)


def _silu(v):
    return v * jax.nn.sigmoid(v)


def _rms_bf16(x, g):
    y = x * lax.rsqrt(jnp.mean(x * x, axis=-1, keepdims=True) + EPS)
    return (y * g).astype(bf16)


def _bias(b_ref, block):
    return b_ref[0, :, block * BRANCH_W:(block + 1) * BRANCH_W]


def _store_slabs(ref, v):
    for s in range(N_SLAB):
        ref[0, s] = v[:, s * LANES:(s + 1) * LANES]


def _kernel_a(x_ref, g_ref, w_fx, w_ca, w_cb, w_sc, w_sh, b_ref, csg_ref,
              xx_ref, c_ref, u_ref, split_ref):
    h = _rms_bf16(x_ref[0], g_ref[0])

    def proj(w_ref, block):
        return _dot(h, w_ref[0]) + _bias(b_ref, block)

    fx = proj(w_fx, _FX).astype(bf16)
    slabs_per_plane = PAIR_W // LANES
    for p in range(BRANCH_W // PAIR_W):
        cols = slice(p * PAIR_W, (p + 1) * PAIR_W)
        xx = _dot(fx[:, cols], csg_ref[...])
        for q in range(2 * slabs_per_plane):
            split_ref[q] = xx[:, q * LANES:(q + 1) * LANES]
        for parity in range(2):
            rows = [split_ref[q, pl.ds(parity, TM // 2, stride=2), :]
                    for q in range(2 * slabs_per_plane)]
            for plane in range(2):
                part = rows[plane * slabs_per_plane:(plane + 1) * slabs_per_plane]
                xx_ref[0, parity, plane, :, cols] = jnp.concatenate(part, axis=1).astype(bf16)
    _store_slabs(c_ref, proj(w_ca, _CA) * jax.nn.sigmoid(proj(w_cb, _CB)))
    _store_slabs(u_ref, proj(w_sc, _SC) * proj(w_sh, _SH))


def _kernel_f(tab_ref, xx_ref, y_ref):
    j = pl.program_id(1)
    k_rows = pl.ds(pl.multiple_of(j * TM, TM), TM)
    even = _dot(tab_ref[0, k_rows, :], xx_ref[0, 0])
    odd = _dot(tab_ref[1, k_rows, :], xx_ref[0, 1])
    y_ref[0, 0] = even + odd
    y_ref[0, 1] = even - odd


def _fill_window(win_ref, top_ref, mid_ref, bot_ref, pad, j):
    for s in range(N_SLAB):
        win_ref[s, 0:pad, :] = jnp.where(j > 0, top_ref[0, s], 0.0)
        win_ref[s, pad:pad + TM, :] = mid_ref[0, s]
        win_ref[s, pad + TM:2 * pad + TM, :] = jnp.where(j < NT - 1, bot_ref[0, s], 0.0)


def _depthwise(wins_ref, z, r0, rows, w_ref, b_ref, taps, pad):
    off = pad - (taps - 1) // 2
    park = TM + 2 * pad
    accs = []
    for s in range(N_SLAB):
        ls = slice(s * LANES, (s + 1) * LANES)
        acc = jnp.broadcast_to(b_ref[0, :, ls], (rows, LANES))
        for k in range(taps):
            acc = acc + w_ref[0, k:k + 1, ls] * wins_ref[z, s, pl.ds(r0 + off + k, rows), :]
        wins_ref[0, s, park:park + rows, :] = acc
        accs.append(wins_ref[0, s, park:park + rows, :])
    return jnp.concatenate(accs, axis=1)


def _kernel_b(x_ref, g_ref, y_ref, c_top, c_mid, c_bot, u_top, u_mid, u_bot,
              w_fz, w_cz, w_sb, w_sz, w_g0a, w_g0b, w_g1a, w_g1b, w_g2a, w_g2b,
              b_ref, ccw_ref, ccb_ref, csw_ref, csb_ref, lng_ref, lnb_ref,
              wbr_ref, wo_ref, fg_ref, o_ref, winc_ref, winu_ref, cv_ref, ys_ref, *, final):
    j = pl.program_id(1)
    _fill_window(winc_ref.at[0], c_top, c_mid, c_bot, CONF_PAD, j)
    _fill_window(winu_ref.at[0], u_top, u_mid, u_bot, SHORT_PAD, j)

    def body(z, carry):
        for q in range(TM // CONV_ROWS):
            rows = slice(q * CONV_ROWS, (q + 1) * CONV_ROWS)
            cv_ref[rows, :] = _depthwise(
                winc_ref, z, q * CONV_ROWS, CONV_ROWS, ccw_ref, ccb_ref, CONF_K, CONF_PAD)
            ys_ref[rows, :] = _depthwise(
                winu_ref, z, q * CONV_ROWS, CONV_ROWS, csw_ref, csb_ref, SHORT_K, SHORT_PAD)

        x = x_ref[0]
        h = _rms_bf16(x, g_ref[0])

        def proj(w_ref, block):
            return _dot(h, w_ref[0]) + _bias(b_ref, block)

        def gate(k, w_lo, w_hi):
            blk = _GATE0 + 2 * k
            return jax.nn.sigmoid(
                jnp.concatenate([proj(w_lo, blk), proj(w_hi, blk + 1)], axis=1))

        y_f = y_ref[0] * _silu(proj(w_fz, _FZ))
        m = gate(0, w_g0a, w_g0b) * _dot(y_f.astype(bf16), wbr_ref[0, 0])

        y_s = proj(w_sb, _SB) * ys_ref[...] * _silu(proj(w_sz, _SZ))
        m = m + gate(2, w_g2a, w_g2b) * _dot(y_s.astype(bf16), wbr_ref[0, 2])

        cv = cv_ref[...]
        cc = cv - jnp.mean(cv, axis=-1, keepdims=True)
        var = jnp.mean(cc * cc, axis=-1, keepdims=True)
        ln = cc * lax.rsqrt(var + EPS) * lng_ref[0] + lnb_ref[0]
        y_c = _silu(ln) * _silu(proj(w_cz, _CZ))
        m = m + gate(1, w_g1a, w_g1b) * _dot(y_c.astype(bf16), wbr_ref[0, 1])

        out = x + _dot(m.astype(bf16), wo_ref[0])
        if final:
            out = out * lax.rsqrt(jnp.mean(out * out, axis=-1, keepdims=True) + EPS) * fg_ref[...]
        o_ref[0] = out
        return carry

    lax.fori_loop(0, jnp.minimum(j + 1, 1), body, 0)


def _resident(shape, *index):
    return pl.BlockSpec(shape, lambda b, j: index, pipeline_mode=pl.Buffered(1))


def _layer_row(width, l):
    return _resident((1, 1, width), l, 0, 0)


def _w_in_block(l, block):
    return _resident((1, D_MODEL, BRANCH_W), l, 0, block)


def _params():
    return pltpu.CompilerParams(dimension_semantics=("arbitrary", "arbitrary"),
                                vmem_limit_bytes=VMEM_LIMIT)


def _tile(width):
    return pl.BlockSpec((1, TM, width), lambda b, j: (b, j, 0))


_SLAB_TILE = pl.BlockSpec((1, N_SLAB, TM, LANES), lambda b, j: (b, 0, j, 0))
_SLAB_SHAPE = jax.ShapeDtypeStruct((BATCH, N_SLAB, SEQ, LANES), f32)


def _halo_specs(pad):
    per_tile = TM // pad
    last = SEQ // pad - 1
    shape = (1, N_SLAB, pad, LANES)
    return [pl.BlockSpec(shape, lambda b, j: (b, 0, jnp.maximum(j * per_tile - 1, 0), 0)),
            _SLAB_TILE,
            pl.BlockSpec(shape, lambda b, j: (b, 0, jnp.minimum((j + 1) * per_tile, last), 0))]


def _call_a(l, x, norm_g, w_in, b_in, csg):
    return pl.pallas_call(
        _kernel_a,
        grid=(BATCH, NT),
        in_specs=[_tile(D_MODEL), _layer_row(D_MODEL, l)]
                 + [_w_in_block(l, blk) for blk in A_BLOCKS]
                 + [_layer_row(IN_W, l), _resident((PAIR_W, 2 * PAIR_W), 0, 0)],
        out_specs=[pl.BlockSpec((1, 2, 2, TM // 2, BRANCH_W), lambda b, j: (b, 0, 0, j, 0)),
                   _SLAB_TILE, _SLAB_TILE],
        out_shape=[jax.ShapeDtypeStruct((BATCH, 2, 2, HALF, BRANCH_W), bf16),
                   _SLAB_SHAPE, _SLAB_SHAPE],
        scratch_shapes=[pltpu.VMEM((2 * PAIR_W // LANES, TM, LANES), f32)],
        compiler_params=_params(),
        name="proj_mix_inputs",
    )(x, norm_g, *([w_in] * len(A_BLOCKS)), b_in, csg)


def _call_f(tab, xx):
    return pl.pallas_call(
        _kernel_f,
        grid=(BATCH, HALF // TM),
        in_specs=[_resident((2, HALF, SEQ), 0, 0, 0),
                  pl.BlockSpec((1, 2, SEQ, BRANCH_W), lambda b, j: (b, 0, 0, 0))],
        out_specs=pl.BlockSpec((1, 2, TM, BRANCH_W), lambda b, j: (b, 0, j, 0)),
        out_shape=jax.ShapeDtypeStruct((BATCH, 2, HALF, BRANCH_W), f32),
        compiler_params=_params(),
        name="seq_dft",
    )(tab, xx)


def _call_b(l, x, norm_g, y, c, u, w_in, b_in, ccw, ccb, csw, csb, lng, lnb, wbr, wo, fg):
    return pl.pallas_call(
        functools.partial(_kernel_b, final=(l == DEPTH - 1)),
        grid=(BATCH, NT),
        in_specs=[_tile(D_MODEL), _layer_row(D_MODEL, l), _tile(BRANCH_W)]
                 + _halo_specs(CONF_PAD) + _halo_specs(SHORT_PAD)
                 + [_w_in_block(l, blk) for blk in B_BLOCKS]
                 + [_layer_row(IN_W, l),
                    _resident((1, CONF_K, BRANCH_W), l, 0, 0), _layer_row(BRANCH_W, l),
                    _resident((1, SHORT_K, BRANCH_W), l, 0, 0), _layer_row(BRANCH_W, l),
                    _layer_row(BRANCH_W, l), _layer_row(BRANCH_W, l),
                    _resident((1, N_BRANCH, BRANCH_W, D_MODEL), l, 0, 0, 0),
                    _resident((1, D_MODEL, D_MODEL), l, 0, 0),
                    _resident((1, D_MODEL), 0, 0)],
        out_specs=_tile(D_MODEL),
        out_shape=jax.ShapeDtypeStruct((BATCH, SEQ, D_MODEL), f32),
        scratch_shapes=[pltpu.VMEM((2, N_SLAB, TM + 2 * CONF_PAD + CONV_ROWS, LANES), f32),
                        pltpu.VMEM((2, N_SLAB, TM + 2 * SHORT_PAD + CONV_ROWS, LANES), f32),
                        pltpu.VMEM((TM, BRANCH_W), f32),
                        pltpu.VMEM((TM, BRANCH_W), f32)],
        compiler_params=_params(),
        name="merge_out",
    )(x, norm_g, y, c, c, c, u, u, u, *([w_in] * len(B_BLOCKS)), b_in,
      ccw, ccb, csw, csb, lng, lnb, wbr, wo, fg)


def kernel(x, norm_g, w_in, b_in, conv_c_w, conv_c_b, ln_c_g, ln_c_b,
           conv_s_w, conv_s_b, w_branch, w_out, final_g):
    tab = jnp.asarray(_SEQ_TABLE).astype(bf16)
    csg = jnp.asarray(_CSG_TABLE).astype(bf16)
    w_in, w_branch, w_out = w_in.astype(bf16), w_branch.astype(bf16), w_out.astype(bf16)
    rows = lambda v: v.reshape(DEPTH, 1, -1)
    norm_g, b_in, conv_c_b, ln_c_g, ln_c_b, conv_s_b = map(
        rows, (norm_g, b_in, conv_c_b, ln_c_g, ln_c_b, conv_s_b))
    final_g = final_g.reshape(1, -1)
    for l in range(DEPTH):
        xx, c, u = _call_a(l, x, norm_g, w_in, b_in, csg)
        xx = xx.reshape(BATCH, 2, SEQ, BRANCH_W)
        y = _call_f(tab, xx).reshape(BATCH, SEQ, BRANCH_W)
        x = _call_b(l, x, norm_g, y, c, u, w_in, b_in, conv_c_w, conv_c_b, conv_s_w, conv_s_b,
                    ln_c_g, ln_c_b, w_branch, w_out, final_g)
    return x
```

```python
import functools

import numpy as np
import jax
import jax.numpy as jnp
from jax import lax
from jax.experimental import pallas as pl
from jax.experimental.pallas import tpu as pltpu

D_MODEL = 1024
BATCH = 8
SEQ = 2048
DEPTH = 2
BRANCH_W = D_MODEL // 2
N_BRANCH = 3
N_GROUPS = 4
GROUP_W = BRANCH_W // N_GROUPS
PAIR_W = 2 * GROUP_W
CONF_K = 31
SHORT_K = 3
EPS = 1e-6
IN_W = 9 * BRANCH_W + N_BRANCH * D_MODEL

TM = 512
NT = SEQ // TM
TM_PROJ = 1024
TM_MERGE = 1024
LANES = 128
N_SLAB = BRANCH_W // LANES
HALF = SEQ // 2
NT_HALF = NT // 2
CONV_ROWS = 32
CONV_PER_TILE = TM // CONV_ROWS
CONF_PAD = 16
SHORT_PAD = 8
VMEM_LIMIT = 60 * 1024 * 1024

_FX, _FZ, _CA, _CB, _CZ, _SB, _SC, _SH, _SZ = range(9)
_GATE0 = 9
A_BLOCKS = (_FX, _CA, _CB, _SC, _SH)
B_BLOCKS = (_FZ, _CZ, _SB, _SZ) + tuple(range(_GATE0, _GATE0 + 2 * N_BRANCH))

bf16 = jnp.bfloat16
f32 = jnp.float32


def _dft_tables():
    k = np.arange(HALF, dtype=np.int64)[:, None]
    m = np.arange(HALF, dtype=np.int64)[None, :]
    seq = []
    for parity in range(2):
        ang = 2.0 * np.pi * (((2 * m + parity) * k) % SEQ).astype(np.float64) / SEQ
        seq.append(np.concatenate([np.cos(ang), -np.sin(ang)], axis=1))
    seq = np.stack(seq)
    c = np.arange(GROUP_W, dtype=np.int64)
    angg = 2.0 * np.pi * ((c[:, None] * c[None, :]) % GROUP_W).astype(np.float64) / GROUP_W
    scale = 1.0 / np.sqrt(float(SEQ * GROUP_W))
    pair = np.eye(PAIR_W // GROUP_W)
    cg = np.kron(pair, np.cos(angg)) * scale
    sg = np.kron(pair, np.sin(angg)) * scale
    return seq.astype(np.float32), np.concatenate([cg, sg], axis=1).astype(np.float32)


_SEQ_TABLE, _CSG_TABLE = _dft_tables()


def _dot(a, b):
    return jnp.dot(a, b, preferred_element_type=f32)


def _silu(v):
    return v * jax.nn.sigmoid(v)


def _rms_bf16(x, g):
    y = x * lax.rsqrt(jnp.mean(x * x, axis=-1, keepdims=True) + EPS)
    return (y * g).astype(bf16)


def _bias(b_ref, block):
    return b_ref[0, :, block * BRANCH_W:(block + 1) * BRANCH_W]


def _store_slabs(ref, v):
    for s in range(N_SLAB):
        ref[0, s] = v[:, s * LANES:(s + 1) * LANES]


def _kernel_a(x_ref, g_ref, w_fx, w_ca, w_cb, w_sc, w_sh, b_ref, csg_ref,
              xx_ref, c_ref, u_ref, split_ref):
    h = _rms_bf16(x_ref[0], g_ref[0])

    def proj(w_ref, block):
        return _dot(h, w_ref[0]) + _bias(b_ref, block)

    fx = proj(w_fx, _FX).astype(bf16)
    slabs_per_plane = PAIR_W // LANES
    for p in range(BRANCH_W // PAIR_W):
        cols = slice(p * PAIR_W, (p + 1) * PAIR_W)
        xx = _dot(fx[:, cols], csg_ref[...])
        for q in range(2 * slabs_per_plane):
            split_ref[q] = xx[:, q * LANES:(q + 1) * LANES]
        for parity in range(2):
            rows = [split_ref[q, pl.ds(parity, TM_PROJ // 2, stride=2), :]
                    for q in range(2 * slabs_per_plane)]
            for plane in range(2):
                part = rows[plane * slabs_per_plane:(plane + 1) * slabs_per_plane]
                xx_ref[0, parity, plane, :, cols] = jnp.concatenate(part, axis=1).astype(bf16)
    _store_slabs(c_ref, proj(w_ca, _CA) * jax.nn.sigmoid(proj(w_cb, _CB)))
    _store_slabs(u_ref, proj(w_sc, _SC) * proj(w_sh, _SH))


def _fill_window(win_ref, top_ref, mid_ref, bot_ref, pad, j):
    for s in range(N_SLAB):
        win_ref[s, 0:pad, :] = jnp.where(j > 0, top_ref[0, s], 0.0)
        win_ref[s, pad:pad + TM, :] = mid_ref[0, s]
        win_ref[s, pad + TM:2 * pad + TM, :] = jnp.where(j < NT - 1, bot_ref[0, s], 0.0)


def _depthwise(wins_ref, half, r0, rows, w_ref, b_ref, taps, pad):
    off = pad - (taps - 1) // 2
    park = TM + 2 * pad
    accs = []
    for s in range(N_SLAB):
        ls = slice(s * LANES, (s + 1) * LANES)
        acc = jnp.broadcast_to(b_ref[0, :, ls], (rows, LANES))
        for k in range(taps):
            acc = acc + w_ref[0, k:k + 1, ls] * wins_ref[half, s, pl.ds(r0 + off + k, rows), :]
        wins_ref[0, s, park:park + rows, :] = acc
        accs.append(wins_ref[0, s, park:park + rows, :])
    return jnp.concatenate(accs, axis=1)


def _kernel_f(tab_ref, xx_ref,
              c0_top, c0_mid, c0_bot, c1_top, c1_mid, c1_bot,
              u0_top, u0_mid, u0_bot, u1_top, u1_mid, u1_bot,
              ccw_ref, ccb_ref, csw_ref, csb_ref,
              y_ref, cv_ref, ys_ref, winc_ref, winu_ref, eo_ref):
    j = pl.program_id(1)
    halos = ((c0_top, c0_mid, c0_bot, u0_top, u0_mid, u0_bot),
             (c1_top, c1_mid, c1_bot, u1_top, u1_mid, u1_bot))
    for half, (ct, cm, cb, ut, um, ub) in enumerate(halos):
        tile = j + half * NT_HALF
        _fill_window(winc_ref.at[half], ct, cm, cb, CONF_PAD, tile)
        _fill_window(winu_ref.at[half], ut, um, ub, SHORT_PAD, tile)
    k_rows = pl.ds(pl.multiple_of(j * TM, TM), TM)

    def body(half, carry):
        eo_ref[half] = _dot(tab_ref[half, k_rows, :], xx_ref[0, half])
        for q in range(CONV_PER_TILE):
            rows = slice(q * CONV_ROWS, (q + 1) * CONV_ROWS)
            cv_ref[0, half, rows, :] = _depthwise(
                winc_ref, half, q * CONV_ROWS, CONV_ROWS, ccw_ref, ccb_ref, CONF_K, CONF_PAD)
            ys_ref[0, half, rows, :] = _depthwise(
                winu_ref, half, q * CONV_ROWS, CONV_ROWS, csw_ref, csb_ref, SHORT_K, SHORT_PAD)
        return carry

    lax.fori_loop(0, jnp.minimum(j + 2, 2), body, 0)
    even, odd = eo_ref[0], eo_ref[1]
    y_ref[0, 0] = even + odd
    y_ref[0, 1] = even - odd


def _kernel_b(x_ref, g_ref, y_ref, cv_ref, ys_ref,
              w_fz, w_cz, w_sb, w_sz, w_g0a, w_g0b, w_g1a, w_g1b, w_g2a, w_g2b,
              b_ref, lng_ref, lnb_ref, wbr_ref, wo_ref, fg_ref, o_ref, *, final):
    x = x_ref[0]
    h = _rms_bf16(x, g_ref[0])

    def proj(w_ref, block):
        return _dot(h, w_ref[0]) + _bias(b_ref, block)

    def gate(k, w_lo, w_hi):
        blk = _GATE0 + 2 * k
        return jax.nn.sigmoid(jnp.concatenate([proj(w_lo, blk), proj(w_hi, blk + 1)], axis=1))

    y_f = y_ref[0] * _silu(proj(w_fz, _FZ))
    m = gate(0, w_g0a, w_g0b) * _dot(y_f.astype(bf16), wbr_ref[0, 0])

    cv = cv_ref[0]
    cc = cv - jnp.mean(cv, axis=-1, keepdims=True)
    var = jnp.mean(cc * cc, axis=-1, keepdims=True)
    ln = cc * lax.rsqrt(var + EPS) * lng_ref[0] + lnb_ref[0]
    y_c = _silu(ln) * _silu(proj(w_cz, _CZ))
    m = m + gate(1, w_g1a, w_g1b) * _dot(y_c.astype(bf16), wbr_ref[0, 1])

    y_s = proj(w_sb, _SB) * ys_ref[0] * _silu(proj(w_sz, _SZ))
    m = m + gate(2, w_g2a, w_g2b) * _dot(y_s.astype(bf16), wbr_ref[0, 2])

    out = x + _dot(m.astype(bf16), wo_ref[0])
    if final:
        out = out * lax.rsqrt(jnp.mean(out * out, axis=-1, keepdims=True) + EPS) * fg_ref[...]
    o_ref[0] = out


def _resident(shape, *index):
    return pl.BlockSpec(shape, lambda b, j: index, pipeline_mode=pl.Buffered(1))


def _layer_row(width, l):
    return _resident((1, 1, width), l, 0, 0)


def _w_in_block(l, block):
    return _resident((1, D_MODEL, BRANCH_W), l, 0, block)


def _params():
    return pltpu.CompilerParams(dimension_semantics=("arbitrary", "arbitrary"),
                                vmem_limit_bytes=VMEM_LIMIT)


def _tile(width, rows=TM):
    return pl.BlockSpec((1, rows, width), lambda b, j: (b, j, 0))


_SLAB_SHAPE = jax.ShapeDtypeStruct((BATCH, N_SLAB, SEQ, LANES), f32)


def _call_a(l, x, norm_g, w_in, b_in, csg):
    slab_tile = pl.BlockSpec((1, N_SLAB, TM_PROJ, LANES), lambda b, j: (b, 0, j, 0))
    return pl.pallas_call(
        _kernel_a,
        grid=(BATCH, SEQ // TM_PROJ),
        in_specs=[_tile(D_MODEL, TM_PROJ), _layer_row(D_MODEL, l)]
                 + [_w_in_block(l, blk) for blk in A_BLOCKS]
                 + [_layer_row(IN_W, l), _resident((PAIR_W, 2 * PAIR_W), 0, 0)],
        out_specs=[pl.BlockSpec((1, 2, 2, TM_PROJ // 2, BRANCH_W), lambda b, j: (b, 0, 0, j, 0)),
                   slab_tile, slab_tile],
        out_shape=[jax.ShapeDtypeStruct((BATCH, 2, 2, HALF, BRANCH_W), bf16),
                   _SLAB_SHAPE, _SLAB_SHAPE],
        scratch_shapes=[pltpu.VMEM((2 * PAIR_W // LANES, TM_PROJ, LANES), f32)],
        compiler_params=_params(),
        name="proj_mix_inputs",
    )(x, norm_g, *([w_in] * len(A_BLOCKS)), b_in, csg)


def _halo_specs(pad, half):
    per_tile = TM // pad
    last = SEQ // pad - 1
    shape = (1, N_SLAB, pad, LANES)
    first_tile = half * NT_HALF
    return [pl.BlockSpec(shape, lambda b, j: (
                b, 0, jnp.maximum((j + first_tile) * per_tile - 1, 0), 0)),
            pl.BlockSpec((1, N_SLAB, TM, LANES), lambda b, j: (b, 0, j + first_tile, 0)),
            pl.BlockSpec(shape, lambda b, j: (
                b, 0, jnp.minimum((j + first_tile + 1) * per_tile, last), 0))]


def _call_f(l, tab, xx, c, u, ccw, ccb, csw, csb):
    halves = pl.BlockSpec((1, 2, TM, BRANCH_W), lambda b, j: (b, 0, j, 0))
    shape = jax.ShapeDtypeStruct((BATCH, 2, HALF, BRANCH_W), f32)
    return pl.pallas_call(
        _kernel_f,
        grid=(BATCH, NT_HALF),
        in_specs=[_resident((2, HALF, SEQ), 0, 0, 0),
                  pl.BlockSpec((1, 2, SEQ, BRANCH_W), lambda b, j: (b, 0, 0, 0))]
                 + _halo_specs(CONF_PAD, 0) + _halo_specs(CONF_PAD, 1)
                 + _halo_specs(SHORT_PAD, 0) + _halo_specs(SHORT_PAD, 1) + [
                  _resident((1, CONF_K, BRANCH_W), l, 0, 0), _layer_row(BRANCH_W, l),
                  _resident((1, SHORT_K, BRANCH_W), l, 0, 0), _layer_row(BRANCH_W, l)],
        out_specs=[halves] * 3,
        out_shape=[shape] * 3,
        scratch_shapes=[pltpu.VMEM((2, N_SLAB, TM + 2 * CONF_PAD + CONV_ROWS, LANES), f32),
                        pltpu.VMEM((2, N_SLAB, TM + 2 * SHORT_PAD + CONV_ROWS, LANES), f32),
                        pltpu.VMEM((2, TM, BRANCH_W), f32)],
        compiler_params=_params(),
        name="seq_mix",
    )(tab, xx, *([c] * 6), *([u] * 6), ccw, ccb, csw, csb)


def _call_b(l, x, norm_g, y, cv, ys, w_in, b_in, lng, lnb, wbr, wo, fg):
    return pl.pallas_call(
        functools.partial(_kernel_b, final=(l == DEPTH - 1)),
        grid=(BATCH, SEQ // TM_MERGE),
        in_specs=[_tile(D_MODEL, TM_MERGE), _layer_row(D_MODEL, l)]
                 + [_tile(BRANCH_W, TM_MERGE)] * 3
                 + [_w_in_block(l, blk) for blk in B_BLOCKS]
                 + [_layer_row(IN_W, l), _layer_row(BRANCH_W, l), _layer_row(BRANCH_W, l),
                    _resident((1, N_BRANCH, BRANCH_W, D_MODEL), l, 0, 0, 0),
                    _resident((1, D_MODEL, D_MODEL), l, 0, 0),
                    _resident((1, D_MODEL), 0, 0)],
        out_specs=_tile(D_MODEL, TM_MERGE),
        out_shape=jax.ShapeDtypeStruct((BATCH, SEQ, D_MODEL), f32),
        compiler_params=_params(),
        name="merge_out",
    )(x, norm_g, y, cv, ys, *([w_in] * len(B_BLOCKS)), b_in, lng, lnb, wbr, wo, fg)


def kernel(x, norm_g, w_in, b_in, conv_c_w, conv_c_b, ln_c_g, ln_c_b,
           conv_s_w, conv_s_b, w_branch, w_out, final_g):
    tab = jnp.asarray(_SEQ_TABLE).astype(bf16)
    csg = jnp.asarray(_CSG_TABLE).astype(bf16)
    w_in, w_branch, w_out = w_in.astype(bf16), w_branch.astype(bf16), w_out.astype(bf16)
    rows = lambda v: v.reshape(DEPTH, 1, -1)
    norm_g, b_in, conv_c_b, ln_c_g, ln_c_b, conv_s_b = map(
        rows, (norm_g, b_in, conv_c_b, ln_c_g, ln_c_b, conv_s_b))
    final_g = final_g.reshape(1, -1)
    for l in range(DEPTH):
        xx, c, u = _call_a(l, x, norm_g, w_in, b_in, csg)
        xx = xx.reshape(BATCH, 2, SEQ, BRANCH_W)
        y, cv, ys = (v.reshape(BATCH, SEQ, BRANCH_W) for v in
                     _call_f(l, tab, xx, c, u, conv_c_w, conv_c_b, conv_s_w, conv_s_b))
        x = _call_b(l, x, norm_g, y, cv, ys, w_in, b_in, ln_c_g, ln_c_b, w_branch, w_out, final_g)
    return x
```

```python
import functools

import numpy as np
import jax
import jax.numpy as jnp
from jax import lax
from jax.experimental import pallas as pl
from jax.experimental.pallas import tpu as pltpu

D_MODEL = 1024
BATCH = 8
SEQ = 2048
DEPTH = 2
BRANCH_W = D_MODEL // 2
N_BRANCH = 3
N_GROUPS = 4
GROUP_W = BRANCH_W // N_GROUPS
PAIR_W = 2 * GROUP_W
CONF_K = 31
SHORT_K = 3
EPS = 1e-6
IN_W = 9 * BRANCH_W + N_BRANCH * D_MODEL

TM = 512
NT = SEQ // TM
TM_PROJ = 1024
TM_MERGE = 1024
LANES = 128
N_SLAB = BRANCH_W // LANES
HALF = SEQ // 2
NT_HALF = NT // 2
CONV_ROWS = 32
SHORT_ROWS = 32
STAGE_ROWS = BRANCH_W
CONF_PAD = 16
SHORT_PAD = 8
VMEM_LIMIT = 60 * 1024 * 1024

_FX, _FZ, _CA, _CB, _CZ, _SB, _SC, _SH, _SZ = range(9)
_GATE0 = 9
A_BLOCKS = (_FX, _CA, _CB, _SC, _SH)
B_BLOCKS = (_FZ, _CZ, _SB, _SZ) + tuple(range(_GATE0, _GATE0 + 2 * N_BRANCH))

bf16 = jnp.bfloat16
f32 = jnp.float32


def _dft_tables():
    k = np.arange(HALF, dtype=np.int64)[:, None]
    m = np.arange(HALF, dtype=np.int64)[None, :]
    seq = []
    for parity in range(2):
        ang = 2.0 * np.pi * (((2 * m + parity) * k) % SEQ).astype(np.float64) / SEQ
        seq.append(np.concatenate([np.cos(ang), -np.sin(ang)], axis=1))
    seq = np.stack(seq)
    c = np.arange(GROUP_W, dtype=np.int64)
    angg = 2.0 * np.pi * ((c[:, None] * c[None, :]) % GROUP_W).astype(np.float64) / GROUP_W
    scale = 1.0 / np.sqrt(float(SEQ * GROUP_W))
    pair = np.eye(PAIR_W // GROUP_W)
    cg = np.kron(pair, np.cos(angg)) * scale
    sg = np.kron(pair, np.sin(angg)) * scale
    return seq.astype(np.float32), np.concatenate([cg, sg], axis=1).astype(np.float32)


_SEQ_TABLE, _CSG_TABLE = _dft_tables()


def _dot(a, b):
    return jnp.dot(a, b, preferred_element_type=f32)


def _silu(v):
    return v * jax.nn.sigmoid(v)


def _rms_bf16(x, g):
    y = x * lax.rsqrt(jnp.mean(x * x, axis=-1, keepdims=True) + EPS)
    return (y * g).astype(bf16)


def _bias(b_ref, block):
    return b_ref[0, :, block * BRANCH_W:(block + 1) * BRANCH_W]


def _store_slabs(ref, v):
    for s in range(N_SLAB):
        ref[0, s] = v[:, s * LANES:(s + 1) * LANES]


def _first_step():
    return (pl.program_id(0) == 0) & (pl.program_id(1) == 0)


def _stage_weights(pairs, stage_ref, sem_ref):
    def copy(i):
        return pltpu.make_async_copy(pairs[i][0], stage_ref.at[i % 2], sem_ref.at[i % 2])

    copy(0).start()
    for i, (_, dst) in enumerate(pairs):
        if i + 1 < len(pairs):
            copy(i + 1).start()
        copy(i).wait()
        dst[...] = stage_ref[i % 2].astype(bf16)


def _w_in_pairs(w_hbm, l, blocks, wq_ref):
    return [(w_hbm.at[l, pl.ds(r, STAGE_ROWS), pl.ds(blk * BRANCH_W, BRANCH_W)],
             wq_ref.at[i, pl.ds(r, STAGE_ROWS), :])
            for i, blk in enumerate(blocks) for r in range(0, D_MODEL, STAGE_ROWS)]


def _kernel_a(x_ref, g_ref, w_hbm, b_ref, csg_ref, xx_ref, c_ref, u_ref,
              split_ref, wq_ref, stage_ref, sem_ref, *, layer):
    @pl.when(_first_step())
    def _():
        _stage_weights(_w_in_pairs(w_hbm, layer, A_BLOCKS, wq_ref), stage_ref, sem_ref)

    h = _rms_bf16(x_ref[0], g_ref[0])

    def proj(block):
        return _dot(h, wq_ref[A_BLOCKS.index(block)]) + _bias(b_ref, block)

    fx = proj(_FX).astype(bf16)
    slabs_per_plane = PAIR_W // LANES
    for p in range(BRANCH_W // PAIR_W):
        cols = slice(p * PAIR_W, (p + 1) * PAIR_W)
        xx = _dot(fx[:, cols], csg_ref[...])
        for q in range(2 * slabs_per_plane):
            split_ref[q] = xx[:, q * LANES:(q + 1) * LANES]
        for parity in range(2):
            rows = [split_ref[q, pl.ds(parity, TM_PROJ // 2, stride=2), :]
                    for q in range(2 * slabs_per_plane)]
            for plane in range(2):
                part = rows[plane * slabs_per_plane:(plane + 1) * slabs_per_plane]
                xx_ref[0, parity, plane, :, cols] = jnp.concatenate(part, axis=1).astype(bf16)
    _store_slabs(c_ref, proj(_CA) * jax.nn.sigmoid(proj(_CB)))
    _store_slabs(u_ref, proj(_SC) * proj(_SH))


def _fill_window(win_ref, top_ref, mid_ref, bot_ref, pad, j):
    for s in range(N_SLAB):
        win_ref[s, 0:pad, :] = jnp.where(j > 0, top_ref[0, s], 0.0)
        win_ref[s, pad:pad + TM, :] = mid_ref[0, s]
        win_ref[s, pad + TM:2 * pad + TM, :] = jnp.where(j < NT - 1, bot_ref[0, s], 0.0)


def _depthwise(wins_ref, half, r0, rows, w_ref, b_ref, taps, pad):
    off = pad - (taps - 1) // 2
    park = TM + 2 * pad
    accs = []
    for s in range(N_SLAB):
        ls = slice(s * LANES, (s + 1) * LANES)
        acc = jnp.broadcast_to(b_ref[0, :, ls], (rows, LANES))
        for k in range(taps):
            acc = acc + w_ref[0, k:k + 1, ls] * wins_ref[half, s, pl.ds(r0 + off + k, rows), :]
        wins_ref[0, s, park:park + rows, :] = acc
        accs.append(wins_ref[0, s, park:park + rows, :])
    return jnp.concatenate(accs, axis=1)


def _kernel_f(tab_ref, xx_ref,
              c0_top, c0_mid, c0_bot, c1_top, c1_mid, c1_bot,
              u0_top, u0_mid, u0_bot, u1_top, u1_mid, u1_bot,
              ccw_ref, ccb_ref, csw_ref, csb_ref,
              y_ref, cv_ref, ys_ref, winc_ref, winu_ref, eo_ref):
    j = pl.program_id(1)
    halos = ((c0_top, c0_mid, c0_bot, u0_top, u0_mid, u0_bot),
             (c1_top, c1_mid, c1_bot, u1_top, u1_mid, u1_bot))
    for half, (ct, cm, cb, ut, um, ub) in enumerate(halos):
        tile = j + half * NT_HALF
        _fill_window(winc_ref.at[half], ct, cm, cb, CONF_PAD, tile)
        _fill_window(winu_ref.at[half], ut, um, ub, SHORT_PAD, tile)
    k_rows = pl.ds(pl.multiple_of(j * TM, TM), TM)

    def body(half, carry):
        eo_ref[half] = _dot(tab_ref[half, k_rows, :], xx_ref[0, half])
        for r0 in range(0, TM, CONV_ROWS):
            cv_ref[0, half, r0:r0 + CONV_ROWS, :] = _depthwise(
                winc_ref, half, r0, CONV_ROWS, ccw_ref, ccb_ref, CONF_K, CONF_PAD)
            if r0 % SHORT_ROWS == 0:
                ys_ref[0, half, r0:r0 + SHORT_ROWS, :] = _depthwise(
                    winu_ref, half, r0, SHORT_ROWS, csw_ref, csb_ref, SHORT_K, SHORT_PAD)
        return carry

    lax.fori_loop(0, jnp.minimum(j + 2, 2), body, 0)
    even, odd = eo_ref[0], eo_ref[1]
    y_ref[0, 0] = even + odd
    y_ref[0, 1] = even - odd


def _kernel_b(x_ref, g_ref, y_ref, cv_ref, ys_ref, w_hbm, wbr_hbm, wo_hbm,
              b_ref, lng_ref, lnb_ref, fg_ref, o_ref,
              wq_ref, wbr_ref, wo_ref, stage_ref, sem_ref, *, layer, final):
    @pl.when(_first_step())
    def _():
        halves = range(0, D_MODEL, BRANCH_W)
        pairs = _w_in_pairs(w_hbm, layer, B_BLOCKS, wq_ref)
        pairs += [(wbr_hbm.at[layer, k, :, pl.ds(c, BRANCH_W)], wbr_ref.at[k, :, pl.ds(c, BRANCH_W)])
                  for k in range(N_BRANCH) for c in halves]
        pairs += [(wo_hbm.at[layer, pl.ds(r, STAGE_ROWS), pl.ds(c, BRANCH_W)],
                   wo_ref.at[pl.ds(r, STAGE_ROWS), pl.ds(c, BRANCH_W)])
                  for r in range(0, D_MODEL, STAGE_ROWS) for c in halves]
        _stage_weights(pairs, stage_ref, sem_ref)

    x = x_ref[0]
    h = _rms_bf16(x, g_ref[0])

    def proj(block):
        return _dot(h, wq_ref[B_BLOCKS.index(block)]) + _bias(b_ref, block)

    def gate(k):
        blk = _GATE0 + 2 * k
        return jax.nn.sigmoid(jnp.concatenate([proj(blk), proj(blk + 1)], axis=1))

    y_f = y_ref[0] * _silu(proj(_FZ))
    m = gate(0) * _dot(y_f.astype(bf16), wbr_ref[0])

    cv = cv_ref[0]
    cc = cv - jnp.mean(cv, axis=-1, keepdims=True)
    var = jnp.mean(cc * cc, axis=-1, keepdims=True)
    ln = cc * lax.rsqrt(var + EPS) * lng_ref[0] + lnb_ref[0]
    y_c = _silu(ln) * _silu(proj(_CZ))
    m = m + gate(1) * _dot(y_c.astype(bf16), wbr_ref[1])

    y_s = proj(_SB) * ys_ref[0] * _silu(proj(_SZ))
    m = m + gate(2) * _dot(y_s.astype(bf16), wbr_ref[2])

    out = x + _dot(m.astype(bf16), wo_ref[...])
    if final:
        out = out * lax.rsqrt(jnp.mean(out * out, axis=-1, keepdims=True) + EPS) * fg_ref[...]
    o_ref[0] = out


def _resident(shape, *index):
    return pl.BlockSpec(shape, lambda b, j: index, pipeline_mode=pl.Buffered(1))


def _layer_row(width, l):
    return _resident((1, 1, width), l, 0, 0)


_HBM = pl.BlockSpec(memory_space=pl.ANY)
_STAGE_SCRATCH = [pltpu.VMEM((2, STAGE_ROWS, BRANCH_W), f32), pltpu.SemaphoreType.DMA((2,))]


def _params():
    return pltpu.CompilerParams(dimension_semantics=("arbitrary", "arbitrary"),
                                vmem_limit_bytes=VMEM_LIMIT)


def _tile(width, rows=TM):
    return pl.BlockSpec((1, rows, width), lambda b, j: (b, j, 0))


_SLAB_SHAPE = jax.ShapeDtypeStruct((BATCH, N_SLAB, SEQ, LANES), f32)


def _call_a(l, x, norm_g, w_in, b_in, csg):
    slab_tile = pl.BlockSpec((1, N_SLAB, TM_PROJ, LANES), lambda b, j: (b, 0, j, 0))
    return pl.pallas_call(
        functools.partial(_kernel_a, layer=l),
        grid=(BATCH, SEQ // TM_PROJ),
        in_specs=[_tile(D_MODEL, TM_PROJ), _layer_row(D_MODEL, l), _HBM,
                  _layer_row(IN_W, l), _resident((PAIR_W, 2 * PAIR_W), 0, 0)],
        out_specs=[pl.BlockSpec((1, 2, 2, TM_PROJ // 2, BRANCH_W), lambda b, j: (b, 0, 0, j, 0)),
                   slab_tile, slab_tile],
        out_shape=[jax.ShapeDtypeStruct((BATCH, 2, 2, HALF, BRANCH_W), bf16),
                   _SLAB_SHAPE, _SLAB_SHAPE],
        scratch_shapes=[pltpu.VMEM((2 * PAIR_W // LANES, TM_PROJ, LANES), f32),
                        pltpu.VMEM((len(A_BLOCKS), D_MODEL, BRANCH_W), bf16)] + _STAGE_SCRATCH,
        compiler_params=_params(),
        name="proj_mix_inputs",
    )(x, norm_g, w_in, b_in, csg)


def _halo_specs(pad, half):
    per_tile = TM // pad
    last = SEQ // pad - 1
    shape = (1, N_SLAB, pad, LANES)
    first_tile = half * NT_HALF
    return [pl.BlockSpec(shape, lambda b, j: (
                b, 0, jnp.maximum((j + first_tile) * per_tile - 1, 0), 0)),
            pl.BlockSpec((1, N_SLAB, TM, LANES), lambda b, j: (b, 0, j + first_tile, 0)),
            pl.BlockSpec(shape, lambda b, j: (
                b, 0, jnp.minimum((j + first_tile + 1) * per_tile, last), 0))]


def _call_f(l, tab, xx, c, u, ccw, ccb, csw, csb):
    halves = pl.BlockSpec((1, 2, TM, BRANCH_W), lambda b, j: (b, 0, j, 0))
    shape = jax.ShapeDtypeStruct((BATCH, 2, HALF, BRANCH_W), f32)
    return pl.pallas_call(
        _kernel_f,
        grid=(BATCH, NT_HALF),
        in_specs=[_resident((2, HALF, SEQ), 0, 0, 0),
                  pl.BlockSpec((1, 2, SEQ, BRANCH_W), lambda b, j: (b, 0, 0, 0))]
                 + _halo_specs(CONF_PAD, 0) + _halo_specs(CONF_PAD, 1)
                 + _halo_specs(SHORT_PAD, 0) + _halo_specs(SHORT_PAD, 1) + [
                  _resident((1, CONF_K, BRANCH_W), l, 0, 0), _layer_row(BRANCH_W, l),
                  _resident((1, SHORT_K, BRANCH_W), l, 0, 0), _layer_row(BRANCH_W, l)],
        out_specs=[halves] * 3,
        out_shape=[shape] * 3,
        scratch_shapes=[pltpu.VMEM((2, N_SLAB, TM + 2 * CONF_PAD + CONV_ROWS, LANES), f32),
                        pltpu.VMEM((2, N_SLAB, TM + 2 * SHORT_PAD + SHORT_ROWS, LANES), f32),
                        pltpu.VMEM((2, TM, BRANCH_W), f32)],
        compiler_params=_params(),
        name="seq_mix",
    )(tab, xx, *([c] * 6), *([u] * 6), ccw, ccb, csw, csb)


def _call_b(l, x, norm_g, y, cv, ys, w_in, b_in, lng, lnb, wbr, wo, fg):
    return pl.pallas_call(
        functools.partial(_kernel_b, layer=l, final=(l == DEPTH - 1)),
        grid=(BATCH, SEQ // TM_MERGE),
        in_specs=[_tile(D_MODEL, TM_MERGE), _layer_row(D_MODEL, l)]
                 + [_tile(BRANCH_W, TM_MERGE)] * 3
                 + [_HBM, _HBM, _HBM,
                    _layer_row(IN_W, l), _layer_row(BRANCH_W, l), _layer_row(BRANCH_W, l),
                    _resident((1, D_MODEL), 0, 0)],
        out_specs=_tile(D_MODEL, TM_MERGE),
        out_shape=jax.ShapeDtypeStruct((BATCH, SEQ, D_MODEL), f32),
        scratch_shapes=[pltpu.VMEM((len(B_BLOCKS), D_MODEL, BRANCH_W), bf16),
                        pltpu.VMEM((N_BRANCH, BRANCH_W, D_MODEL), bf16),
                        pltpu.VMEM((D_MODEL, D_MODEL), bf16)] + _STAGE_SCRATCH,
        compiler_params=_params(),
        name="merge_out",
    )(x, norm_g, y, cv, ys, w_in, wbr, wo, b_in, lng, lnb, fg)


def kernel(x, norm_g, w_in, b_in, conv_c_w, conv_c_b, ln_c_g, ln_c_b,
           conv_s_w, conv_s_b, w_branch, w_out, final_g):
    tab = jnp.asarray(_SEQ_TABLE).astype(bf16)
    csg = jnp.asarray(_CSG_TABLE).astype(bf16)
    rows = lambda v: v.reshape(DEPTH, 1, -1)
    norm_g, b_in, conv_c_b, ln_c_g, ln_c_b, conv_s_b = map(
        rows, (norm_g, b_in, conv_c_b, ln_c_g, ln_c_b, conv_s_b))
    final_g = final_g.reshape(1, -1)
    for l in range(DEPTH):
        xx, c, u = _call_a(l, x, norm_g, w_in, b_in, csg)
        xx = xx.reshape(BATCH, 2, SEQ, BRANCH_W)
        y, cv, ys = (v.reshape(BATCH, SEQ, BRANCH_W) for v in
                     _call_f(l, tab, xx, c, u, conv_c_w, conv_c_b, conv_s_w, conv_s_b))
        x = _call_b(l, x, norm_g, y, cv, ys, w_in, b_in, ln_c_g, ln_c_b, w_branch, w_out, final_g)
    return x
```

```python
import functools

import numpy as np
import jax
import jax.numpy as jnp
from jax import lax
from jax.experimental import pallas as pl
from jax.experimental.pallas import tpu as pltpu

D_MODEL = 1024
BATCH = 8
SEQ = 2048
DEPTH = 2
BRANCH_W = D_MODEL // 2
N_BRANCH = 3
N_GROUPS = 4
GROUP_W = BRANCH_W // N_GROUPS
PAIR_W = 2 * GROUP_W
CONF_K = 31
SHORT_K = 3
EPS = 1e-6
IN_W = 9 * BRANCH_W + N_BRANCH * D_MODEL

TM = 512
NT = SEQ // TM
TM_PROJ = 1024
TM_MERGE = 1024
LANES = 128
N_SLAB = BRANCH_W // LANES
HALF = SEQ // 2
NT_HALF = NT // 2
CONV_ROWS = 32
SHORT_ROWS = 32
STAGE_ROWS = BRANCH_W
STAGE_SLOTS = 4
CONF_PAD = 16
SHORT_PAD = 8
VMEM_LIMIT = 63 * 1024 * 1024

_FX, _FZ, _CA, _CB, _CZ, _SB, _SC, _SH, _SZ = range(9)
_GATE0 = 9
A_BLOCKS = (_FX, _CA, _CB, _SC, _SH)
B_BLOCKS = (_FZ, _CZ, _SB, _SZ) + tuple(range(_GATE0, _GATE0 + 2 * N_BRANCH))

bf16 = jnp.bfloat16
f32 = jnp.float32


def _dft_tables():
    k = np.arange(HALF, dtype=np.int64)[:, None]
    m = np.arange(HALF, dtype=np.int64)[None, :]
    seq = []
    for parity in range(2):
        ang = 2.0 * np.pi * (((2 * m + parity) * k) % SEQ).astype(np.float64) / SEQ
        seq.append(np.concatenate([np.cos(ang), -np.sin(ang)], axis=1))
    seq = np.stack(seq)
    c = np.arange(GROUP_W, dtype=np.int64)
    angg = 2.0 * np.pi * ((c[:, None] * c[None, :]) % GROUP_W).astype(np.float64) / GROUP_W
    scale = 1.0 / np.sqrt(float(SEQ * GROUP_W))
    pair = np.eye(PAIR_W // GROUP_W)
    cg = np.kron(pair, np.cos(angg)) * scale
    sg = np.kron(pair, np.sin(angg)) * scale
    return seq.astype(np.float32), np.concatenate([cg, sg], axis=1).astype(np.float32)


_SEQ_TABLE, _CSG_TABLE = _dft_tables()


def _dot(a, b):
    return jnp.dot(a, b, preferred_element_type=f32)


def _silu(v):
    return v * jax.nn.sigmoid(v)


def _rms_bf16(x, g):
    y = x * lax.rsqrt(jnp.mean(x * x, axis=-1, keepdims=True) + EPS)
    return (y * g).astype(bf16)


def _store_slabs(ref, v):
    for s in range(N_SLAB):
        ref[0, s] = v[:, s * LANES:(s + 1) * LANES]


def _first_step():
    return (pl.program_id(0) == 0) & (pl.program_id(1) == 0)


def _stage_weights(pairs, stage_ref, sem_ref):
    def copy(i):
        slot = i % STAGE_SLOTS
        return pltpu.make_async_copy(pairs[i][0], stage_ref.at[slot], sem_ref.at[slot])

    for i in range(min(STAGE_SLOTS - 1, len(pairs))):
        copy(i).start()
    for i, (_, dst) in enumerate(pairs):
        if i + STAGE_SLOTS - 1 < len(pairs):
            copy(i + STAGE_SLOTS - 1).start()
        copy(i).wait()
        dst[...] = stage_ref[i % STAGE_SLOTS].astype(bf16)


def _w_in_pairs(w_hbm, l, blocks, wq_ref):
    return [(w_hbm.at[l, pl.ds(r, STAGE_ROWS), pl.ds(blk * BRANCH_W, BRANCH_W)],
             wq_ref.at[pl.ds(r, STAGE_ROWS), pl.ds(i * BRANCH_W, BRANCH_W)])
            for i, blk in enumerate(blocks) for r in range(0, D_MODEL, STAGE_ROWS)]


def _project(h, wq_ref, b_ref, blocks, block, n=1):
    i = blocks.index(block)
    assert blocks[i:i + n] == tuple(range(block, block + n))
    return (_dot(h, wq_ref[:, i * BRANCH_W:(i + n) * BRANCH_W])
            + b_ref[0, :, block * BRANCH_W:(block + n) * BRANCH_W])


def _kernel_a(x_ref, g_ref, w_hbm, b_ref, csg_ref, h_ref, xx_ref, c_ref, u_ref,
              split_ref, wq_ref, stage_ref, sem_ref, *, layer):
    @pl.when(_first_step())
    def _():
        _stage_weights(_w_in_pairs(w_hbm, layer, A_BLOCKS, wq_ref), stage_ref, sem_ref)

    h = _rms_bf16(x_ref[0], g_ref[0])
    h_ref[0] = h

    proj = functools.partial(_project, h, wq_ref, b_ref, A_BLOCKS)

    fx = proj(_FX).astype(bf16)
    slabs_per_plane = PAIR_W // LANES
    for p in range(BRANCH_W // PAIR_W):
        cols = slice(p * PAIR_W, (p + 1) * PAIR_W)
        xx = _dot(fx[:, cols], csg_ref[...])
        for q in range(2 * slabs_per_plane):
            split_ref[q] = xx[:, q * LANES:(q + 1) * LANES]
        for parity in range(2):
            rows = [split_ref[q, pl.ds(parity, TM_PROJ // 2, stride=2), :]
                    for q in range(2 * slabs_per_plane)]
            for plane in range(2):
                part = rows[plane * slabs_per_plane:(plane + 1) * slabs_per_plane]
                xx_ref[0, parity, plane, :, cols] = jnp.concatenate(part, axis=1).astype(bf16)
    _store_slabs(c_ref, proj(_CA) * jax.nn.sigmoid(proj(_CB)))
    _store_slabs(u_ref, proj(_SC) * proj(_SH))


def _fill_window(win_ref, top_ref, mid_ref, bot_ref, pad, j):
    for s in range(N_SLAB):
        win_ref[s, 0:pad, :] = jnp.where(j > 0, top_ref[0, s], 0.0)
        win_ref[s, pad:pad + TM, :] = mid_ref[0, s]
        win_ref[s, pad + TM:2 * pad + TM, :] = jnp.where(j < NT - 1, bot_ref[0, s], 0.0)


def _depthwise(wins_ref, half, r0, rows, w_ref, b_ref, taps, pad):
    off = pad - (taps - 1) // 2
    park = TM + 2 * pad
    accs = []
    for s in range(N_SLAB):
        ls = slice(s * LANES, (s + 1) * LANES)
        acc = jnp.broadcast_to(b_ref[0, :, ls], (rows, LANES))
        for k in range(taps):
            acc = acc + w_ref[0, k:k + 1, ls] * wins_ref[half, s, pl.ds(r0 + off + k, rows), :]
        wins_ref[0, s, park:park + rows, :] = acc
        accs.append(wins_ref[0, s, park:park + rows, :])
    return jnp.concatenate(accs, axis=1)


def _kernel_f(tab_ref, xx_ref,
              c0_top, c0_mid, c0_bot, c1_top, c1_mid, c1_bot,
              u0_top, u0_mid, u0_bot, u1_top, u1_mid, u1_bot,
              ccw_ref, ccb_ref, csw_ref, csb_ref,
              y_ref, cv_ref, ys_ref, winc_ref, winu_ref, eo_ref):
    j = pl.program_id(1)
    halos = ((c0_top, c0_mid, c0_bot, u0_top, u0_mid, u0_bot),
             (c1_top, c1_mid, c1_bot, u1_top, u1_mid, u1_bot))
    for half, (ct, cm, cb, ut, um, ub) in enumerate(halos):
        tile = j + half * NT_HALF
        _fill_window(winc_ref.at[half], ct, cm, cb, CONF_PAD, tile)
        _fill_window(winu_ref.at[half], ut, um, ub, SHORT_PAD, tile)
    k_rows = pl.ds(pl.multiple_of(j * TM, TM), TM)

    def body(half, carry):
        eo_ref[half] = _dot(tab_ref[half, k_rows, :], xx_ref[0, half])
        for r0 in range(0, TM, CONV_ROWS):
            cv_ref[0, half, r0:r0 + CONV_ROWS, :] = _depthwise(
                winc_ref, half, r0, CONV_ROWS, ccw_ref, ccb_ref, CONF_K, CONF_PAD)
            if r0 % SHORT_ROWS == 0:
                ys_ref[0, half, r0:r0 + SHORT_ROWS, :] = _depthwise(
                    winu_ref, half, r0, SHORT_ROWS, csw_ref, csb_ref, SHORT_K, SHORT_PAD)
        return carry

    lax.fori_loop(0, jnp.minimum(j + 2, 2), body, 0)
    even, odd = eo_ref[0], eo_ref[1]
    y_ref[0, 0] = even + odd
    y_ref[0, 1] = even - odd


def _kernel_b(x_ref, h_ref, y_ref, cv_ref, ys_ref, w_hbm, wbr_hbm, wo_hbm,
              b_ref, lng_ref, lnb_ref, fg_ref, o_ref,
              wq_ref, wbr_ref, wo_ref, stage_ref, sem_ref, *, layer, final):
    @pl.when(_first_step())
    def _():
        halves = range(0, D_MODEL, BRANCH_W)
        pairs = _w_in_pairs(w_hbm, layer, B_BLOCKS, wq_ref)
        pairs += [(wbr_hbm.at[layer, k, :, pl.ds(c, BRANCH_W)], wbr_ref.at[k, :, pl.ds(c, BRANCH_W)])
                  for k in range(N_BRANCH) for c in halves]
        pairs += [(wo_hbm.at[layer, pl.ds(r, STAGE_ROWS), pl.ds(c, BRANCH_W)],
                   wo_ref.at[pl.ds(r, STAGE_ROWS), pl.ds(c, BRANCH_W)])
                  for r in range(0, D_MODEL, STAGE_ROWS) for c in halves]
        _stage_weights(pairs, stage_ref, sem_ref)

    x = x_ref[0]

    def proj(block, n=1):
        return _project(h_ref[0], wq_ref, b_ref, B_BLOCKS, block, n)

    def gate(k):
        return jax.nn.sigmoid(proj(_GATE0 + 2 * k, 2))

    y_f = y_ref[0] * _silu(proj(_FZ))
    m = gate(0) * _dot(y_f.astype(bf16), wbr_ref[0])

    cv = cv_ref[0]
    cc = cv - jnp.mean(cv, axis=-1, keepdims=True)
    var = jnp.mean(cc * cc, axis=-1, keepdims=True)
    ln = cc * lax.rsqrt(var + EPS) * lng_ref[0] + lnb_ref[0]
    y_c = _silu(ln) * _silu(proj(_CZ))
    m = m + gate(1) * _dot(y_c.astype(bf16), wbr_ref[1])

    y_s = proj(_SB) * ys_ref[0] * _silu(proj(_SZ))
    m = m + gate(2) * _dot(y_s.astype(bf16), wbr_ref[2])

    mb = m.astype(bf16)
    for r in range(0, TM_MERGE, TM_MERGE // 2):
        rows = slice(r, r + TM_MERGE // 2)
        out = x[rows] + _dot(mb[rows], wo_ref[...])
        if final:
            out = out * lax.rsqrt(jnp.mean(out * out, axis=-1, keepdims=True) + EPS) * fg_ref[...]
        o_ref[0, rows, :] = out


def _resident(shape, *index):
    return pl.BlockSpec(shape, lambda b, j: index, pipeline_mode=pl.Buffered(1))


def _layer_row(width, l):
    return _resident((1, 1, width), l, 0, 0)


_HBM = pl.BlockSpec(memory_space=pl.ANY)
_STAGE_SCRATCH = [pltpu.VMEM((STAGE_SLOTS, STAGE_ROWS, BRANCH_W), f32),
                  pltpu.SemaphoreType.DMA((STAGE_SLOTS,))]


def _params():
    return pltpu.CompilerParams(dimension_semantics=("arbitrary", "arbitrary"),
                                vmem_limit_bytes=VMEM_LIMIT)


def _tile(width, rows=TM):
    return pl.BlockSpec((1, rows, width), lambda b, j: (b, j, 0))


_SLAB_SHAPE = jax.ShapeDtypeStruct((BATCH, N_SLAB, SEQ, LANES), f32)


def _call_a(l, x, norm_g, w_in, b_in, csg):
    slab_tile = pl.BlockSpec((1, N_SLAB, TM_PROJ, LANES), lambda b, j: (b, 0, j, 0))
    return pl.pallas_call(
        functools.partial(_kernel_a, layer=l),
        grid=(BATCH, SEQ // TM_PROJ),
        in_specs=[_tile(D_MODEL, TM_PROJ), _layer_row(D_MODEL, l), _HBM,
                  _layer_row(IN_W, l), _resident((PAIR_W, 2 * PAIR_W), 0, 0)],
        out_specs=[_tile(D_MODEL, TM_PROJ),
                   pl.BlockSpec((1, 2, 2, TM_PROJ // 2, BRANCH_W), lambda b, j: (b, 0, 0, j, 0)),
                   slab_tile, slab_tile],
        out_shape=[jax.ShapeDtypeStruct((BATCH, SEQ, D_MODEL), bf16),
                   jax.ShapeDtypeStruct((BATCH, 2, 2, HALF, BRANCH_W), bf16),
                   _SLAB_SHAPE, _SLAB_SHAPE],
        scratch_shapes=[pltpu.VMEM((2 * PAIR_W // LANES, TM_PROJ, LANES), f32),
                        pltpu.VMEM((D_MODEL, len(A_BLOCKS) * BRANCH_W), bf16)] + _STAGE_SCRATCH,
        compiler_params=_params(),
        name="proj_mix_inputs",
    )(x, norm_g, w_in, b_in, csg)


def _halo_specs(pad, half):
    per_tile = TM // pad
    last = SEQ // pad - 1
    shape = (1, N_SLAB, pad, LANES)
    first_tile = half * NT_HALF
    return [pl.BlockSpec(shape, lambda b, j: (
                b, 0, jnp.maximum((j + first_tile) * per_tile - 1, 0), 0)),
            pl.BlockSpec((1, N_SLAB, TM, LANES), lambda b, j: (b, 0, j + first_tile, 0)),
            pl.BlockSpec(shape, lambda b, j: (
                b, 0, jnp.minimum((j + first_tile + 1) * per_tile, last), 0))]


def _call_f(l, tab, xx, c, u, ccw, ccb, csw, csb):
    halves = pl.BlockSpec((1, 2, TM, BRANCH_W), lambda b, j: (b, 0, j, 0))
    shape = jax.ShapeDtypeStruct((BATCH, 2, HALF, BRANCH_W), f32)
    return pl.pallas_call(
        _kernel_f,
        grid=(BATCH, NT_HALF),
        in_specs=[_resident((2, HALF, SEQ), 0, 0, 0),
                  pl.BlockSpec((1, 2, SEQ, BRANCH_W), lambda b, j: (b, 0, 0, 0))]
                 + _halo_specs(CONF_PAD, 0) + _halo_specs(CONF_PAD, 1)
                 + _halo_specs(SHORT_PAD, 0) + _halo_specs(SHORT_PAD, 1) + [
                  _resident((1, CONF_K, BRANCH_W), l, 0, 0), _layer_row(BRANCH_W, l),
                  _resident((1, SHORT_K, BRANCH_W), l, 0, 0), _layer_row(BRANCH_W, l)],
        out_specs=[halves] * 3,
        out_shape=[shape] * 3,
        scratch_shapes=[pltpu.VMEM((2, N_SLAB, TM + 2 * CONF_PAD + CONV_ROWS, LANES), f32),
                        pltpu.VMEM((2, N_SLAB, TM + 2 * SHORT_PAD + SHORT_ROWS, LANES), f32),
                        pltpu.VMEM((2, TM, BRANCH_W), f32)],
        compiler_params=_params(),
        name="seq_mix",
    )(tab, xx, *([c] * 6), *([u] * 6), ccw, ccb, csw, csb)


def _call_b(l, x, h, y, cv, ys, w_in, b_in, lng, lnb, wbr, wo, fg):
    return pl.pallas_call(
        functools.partial(_kernel_b, layer=l, final=(l == DEPTH - 1)),
        grid=(BATCH, SEQ // TM_MERGE),
        in_specs=[_tile(D_MODEL, TM_MERGE), _tile(D_MODEL, TM_MERGE)]
                 + [_tile(BRANCH_W, TM_MERGE)] * 3
                 + [_HBM, _HBM, _HBM,
                    _layer_row(IN_W, l), _layer_row(BRANCH_W, l), _layer_row(BRANCH_W, l),
                    _resident((1, D_MODEL), 0, 0)],
        out_specs=_tile(D_MODEL, TM_MERGE),
        out_shape=jax.ShapeDtypeStruct((BATCH, SEQ, D_MODEL), f32),
        scratch_shapes=[pltpu.VMEM((D_MODEL, len(B_BLOCKS) * BRANCH_W), bf16),
                        pltpu.VMEM((N_BRANCH, BRANCH_W, D_MODEL), bf16),
                        pltpu.VMEM((D_MODEL, D_MODEL), bf16)] + _STAGE_SCRATCH,
        compiler_params=_params(),
        name="merge_out",
    )(x, h, y, cv, ys, w_in, wbr, wo, b_in, lng, lnb, fg)


def kernel(x, norm_g, w_in, b_in, conv_c_w, conv_c_b, ln_c_g, ln_c_b,
           conv_s_w, conv_s_b, w_branch, w_out, final_g):
    tab = jnp.asarray(_SEQ_TABLE).astype(bf16)
    csg = jnp.asarray(_CSG_TABLE).astype(bf16)
    rows = lambda v: v.reshape(DEPTH, 1, -1)
    norm_g, b_in, conv_c_b, ln_c_g, ln_c_b, conv_s_b = map(
        rows, (norm_g, b_in, conv_c_b, ln_c_g, ln_c_b, conv_s_b))
    final_g = final_g.reshape(1, -1)
    for l in range(DEPTH):
        h, xx, c, u = _call_a(l, x, norm_g, w_in, b_in, csg)
        xx = xx.reshape(BATCH, 2, SEQ, BRANCH_W)
        y, cv, ys = (v.reshape(BATCH, SEQ, BRANCH_W) for v in
                     _call_f(l, tab, xx, c, u, conv_c_w, conv_c_b, conv_s_w, conv_s_b))
        x = _call_b(l, x, h, y, cv, ys, w_in, b_in, ln_c_g, ln_c_b, w_branch, w_out, final_g)
    return x
```

```python
import functools

import numpy as np
import jax
import jax.numpy as jnp
from jax import lax
from jax.experimental import pallas as pl
from jax.experimental.pallas import tpu as pltpu

D_MODEL = 1024
BATCH = 8
SEQ = 2048
DEPTH = 2
BRANCH_W = D_MODEL // 2
N_BRANCH = 3
N_GROUPS = 4
GROUP_W = BRANCH_W // N_GROUPS
PAIR_W = 2 * GROUP_W
CONF_K = 31
SHORT_K = 3
EPS = 1e-6
IN_W = 9 * BRANCH_W + N_BRANCH * D_MODEL

TM = 512
NT = SEQ // TM
TM_PROJ = 1024
TM_MERGE = 1024
LANES = 128
N_SLAB = BRANCH_W // LANES
HALF = SEQ // 2
NT_HALF = NT // 2
CONV_ROWS = 32
STAGE_ROWS = BRANCH_W
STAGE_SLOTS = 4
CONF_PAD = 16
SHORT_PAD = 8
VMEM_LIMIT = 63 * 1024 * 1024

_FX, _FZ, _CA, _CB, _CZ, _SB, _SC, _SH, _SZ = range(9)
_GATE0 = 9
A_BLOCKS = (_FX, _CA, _CB, _SC, _SH)
B_BLOCKS = (_FZ, _CZ, _SB, _SZ) + tuple(range(_GATE0, _GATE0 + 2 * N_BRANCH))

bf16 = jnp.bfloat16
f32 = jnp.float32


def _dft_tables():
    k = np.arange(HALF, dtype=np.int64)[:, None]
    m = np.arange(HALF, dtype=np.int64)[None, :]
    seq = []
    for parity in range(2):
        ang = 2.0 * np.pi * (((2 * m + parity) * k) % SEQ).astype(np.float64) / SEQ
        seq.append(np.concatenate([np.cos(ang), -np.sin(ang)], axis=1))
    seq = np.stack(seq)
    c = np.arange(GROUP_W, dtype=np.int64)
    angg = 2.0 * np.pi * ((c[:, None] * c[None, :]) % GROUP_W).astype(np.float64) / GROUP_W
    scale = 1.0 / np.sqrt(float(SEQ * GROUP_W))
    pair = np.eye(PAIR_W // GROUP_W)
    cg = np.kron(pair, np.cos(angg)) * scale
    sg = np.kron(pair, np.sin(angg)) * scale
    return seq.astype(np.float32), np.concatenate([cg, sg], axis=1).astype(np.float32)


_SEQ_TABLE, _CSG_TABLE = _dft_tables()


def _dot(a, b):
    return jnp.dot(a, b, preferred_element_type=f32)


def _silu(v):
    return v * jax.nn.sigmoid(v)


def _rms_bf16(x, g):
    y = x * lax.rsqrt(jnp.mean(x * x, axis=-1, keepdims=True) + EPS)
    return (y * g).astype(bf16)


def _store_slabs(ref, v):
    for s in range(N_SLAB):
        ref[0, s] = v[:, s * LANES:(s + 1) * LANES]


def _first_step():
    return (pl.program_id(0) == 0) & (pl.program_id(1) == 0)


def _stage_weights(pairs, stage_ref, sem_ref):
    def copy(i):
        slot = i % STAGE_SLOTS
        return pltpu.make_async_copy(pairs[i][0], stage_ref.at[slot], sem_ref.at[slot])

    for i in range(min(STAGE_SLOTS - 1, len(pairs))):
        copy(i).start()
    for i, (_, dst) in enumerate(pairs):
        if i + STAGE_SLOTS - 1 < len(pairs):
            copy(i + STAGE_SLOTS - 1).start()
        copy(i).wait()
        dst[...] = stage_ref[i % STAGE_SLOTS].astype(bf16)


def _w_in_pairs(w_hbm, l, blocks, wq_ref):
    return [(w_hbm.at[l, pl.ds(r, STAGE_ROWS), pl.ds(blk * BRANCH_W, BRANCH_W)],
             wq_ref.at[pl.ds(r, STAGE_ROWS), pl.ds(i * BRANCH_W, BRANCH_W)])
            for i, blk in enumerate(blocks) for r in range(0, D_MODEL, STAGE_ROWS)]


def _project(h, wq_ref, b_ref, blocks, block, n=1):
    i = blocks.index(block)
    assert blocks[i:i + n] == tuple(range(block, block + n))
    return (_dot(h, wq_ref[:, i * BRANCH_W:(i + n) * BRANCH_W])
            + b_ref[0, :, block * BRANCH_W:(block + n) * BRANCH_W])


def _kernel_a(x_ref, g_ref, w_hbm, b_ref, csg_ref, csw_ref, csb_ref,
              h_ref, xx_ref, c_ref, ys_ref, edge_ref,
              split_ref, uwin_ref, wq_ref, stage_ref, sem_ref, *, layer):
    @pl.when(_first_step())
    def _():
        _stage_weights(_w_in_pairs(w_hbm, layer, A_BLOCKS, wq_ref), stage_ref, sem_ref)

    h = _rms_bf16(x_ref[0], g_ref[0])
    h_ref[0] = h

    proj = functools.partial(_project, h, wq_ref, b_ref, A_BLOCKS)

    fx = proj(_FX).astype(bf16)
    slabs_per_plane = PAIR_W // LANES
    for p in range(BRANCH_W // PAIR_W):
        cols = slice(p * PAIR_W, (p + 1) * PAIR_W)
        xx = _dot(fx[:, cols], csg_ref[...])
        for q in range(2 * slabs_per_plane):
            split_ref[q] = xx[:, q * LANES:(q + 1) * LANES]
        for parity in range(2):
            rows = [split_ref[q, pl.ds(parity, TM_PROJ // 2, stride=2), :]
                    for q in range(2 * slabs_per_plane)]
            for plane in range(2):
                part = rows[plane * slabs_per_plane:(plane + 1) * slabs_per_plane]
                xx_ref[0, parity, plane, :, cols] = jnp.concatenate(part, axis=1).astype(bf16)
    _store_slabs(c_ref, proj(_CA) * jax.nn.sigmoid(proj(_CB)))

    u = proj(_SC) * proj(_SH)
    edge_ref[0, 0, 0] = u[:SHORT_PAD]
    edge_ref[0, 0, 1] = u[TM_PROJ - SHORT_PAD:]
    zeros = jnp.zeros((SHORT_PAD, LANES), f32)
    slabs = []
    for s in range(N_SLAB):
        ls = slice(s * LANES, (s + 1) * LANES)
        uwin_ref[s, 0:SHORT_PAD, :] = zeros
        uwin_ref[s, SHORT_PAD:SHORT_PAD + TM_PROJ, :] = u[:, ls]
        uwin_ref[s, SHORT_PAD + TM_PROJ:, :] = zeros
        acc = jnp.broadcast_to(csb_ref[0, :, ls], (TM_PROJ, LANES))
        for k in range(SHORT_K):
            shift = SHORT_PAD - (SHORT_K - 1) // 2 + k
            acc = acc + csw_ref[0, k:k + 1, ls] * uwin_ref[s, shift:shift + TM_PROJ, :]
        slabs.append(acc)
    ys_ref[0] = jnp.concatenate(slabs, axis=1)


def _fill_window(win_ref, top_ref, mid_ref, bot_ref, pad, j):
    for s in range(N_SLAB):
        win_ref[s, 0:pad, :] = jnp.where(j > 0, top_ref[0, s], 0.0)
        win_ref[s, pad:pad + TM, :] = mid_ref[0, s]
        win_ref[s, pad + TM:2 * pad + TM, :] = jnp.where(j < NT - 1, bot_ref[0, s], 0.0)


def _depthwise(wins_ref, half, r0, rows, w_ref, b_ref, taps, pad):
    off = pad - (taps - 1) // 2
    park = TM + 2 * pad
    accs = []
    for s in range(N_SLAB):
        ls = slice(s * LANES, (s + 1) * LANES)
        acc = jnp.broadcast_to(b_ref[0, :, ls], (rows, LANES))
        for k in range(taps):
            acc = acc + w_ref[0, k:k + 1, ls] * wins_ref[half, s, pl.ds(r0 + off + k, rows), :]
        wins_ref[0, s, park:park + rows, :] = acc
        accs.append(wins_ref[0, s, park:park + rows, :])
    return jnp.concatenate(accs, axis=1)


def _kernel_f(tab_ref, xx_ref, c0_top, c0_mid, c0_bot, c1_top, c1_mid, c1_bot,
              ccw_ref, ccb_ref, y_ref, cv_ref, winc_ref, eo_ref):
    j = pl.program_id(1)
    for half, (ct, cm, cb) in enumerate(((c0_top, c0_mid, c0_bot), (c1_top, c1_mid, c1_bot))):
        _fill_window(winc_ref.at[half], ct, cm, cb, CONF_PAD, j + half * NT_HALF)
    k_rows = pl.ds(pl.multiple_of(j * TM, TM), TM)

    def body(half, carry):
        eo_ref[half] = _dot(tab_ref[half, k_rows, :], xx_ref[0, half])
        for r0 in range(0, TM, CONV_ROWS):
            cv_ref[0, half, r0:r0 + CONV_ROWS, :] = _depthwise(
                winc_ref, half, r0, CONV_ROWS, ccw_ref, ccb_ref, CONF_K, CONF_PAD)
        return carry

    lax.fori_loop(0, jnp.minimum(j + 2, 2), body, 0)
    even, odd = eo_ref[0], eo_ref[1]
    y_ref[0, 0] = even + odd
    y_ref[0, 1] = even - odd


def _short_conv_tile(ys_ref, prev_ref, next_ref, csw_ref):
    j, last = pl.program_id(1), SEQ // TM_MERGE - 1
    row = lax.broadcasted_iota(jnp.int32, (SHORT_PAD, 1), 0)
    before = csw_ref[0, 0:1, :] * prev_ref[0, 0, 1, SHORT_PAD - 1:SHORT_PAD, :]
    after = csw_ref[0, SHORT_K - 1:SHORT_K, :] * next_ref[0, 0, 0, 0:1, :]
    ys = ys_ref[0]
    top = ys[:SHORT_PAD] + jnp.where((row == 0) & (j > 0), before, 0.0)
    bot = ys[TM_MERGE - SHORT_PAD:] + jnp.where((row == SHORT_PAD - 1) & (j < last), after, 0.0)
    return jnp.concatenate([top, ys[SHORT_PAD:TM_MERGE - SHORT_PAD], bot], axis=0)


def _kernel_b(x_ref, h_ref, y_ref, cv_ref, ys_ref, prev_ref, next_ref, csw_ref,
              w_hbm, wbr_hbm, wo_hbm, b_ref, lng_ref, lnb_ref, fg_ref, o_ref,
              wq_ref, wbr_ref, wo_ref, stage_ref, sem_ref, *, layer, final):
    @pl.when(_first_step())
    def _():
        halves = range(0, D_MODEL, BRANCH_W)
        pairs = _w_in_pairs(w_hbm, layer, B_BLOCKS, wq_ref)
        pairs += [(wbr_hbm.at[layer, k, :, pl.ds(c, BRANCH_W)], wbr_ref.at[k, :, pl.ds(c, BRANCH_W)])
                  for k in range(N_BRANCH) for c in halves]
        pairs += [(wo_hbm.at[layer, pl.ds(r, STAGE_ROWS), pl.ds(c, BRANCH_W)],
                   wo_ref.at[pl.ds(r, STAGE_ROWS), pl.ds(c, BRANCH_W)])
                  for r in range(0, D_MODEL, STAGE_ROWS) for c in halves]
        _stage_weights(pairs, stage_ref, sem_ref)

    x = x_ref[0]

    def proj(block, n=1):
        return _project(h_ref[0], wq_ref, b_ref, B_BLOCKS, block, n)

    def gate(k):
        return jax.nn.sigmoid(proj(_GATE0 + 2 * k, 2))

    y_f = y_ref[0] * _silu(proj(_FZ))
    m = gate(0) * _dot(y_f.astype(bf16), wbr_ref[0])

    cv = cv_ref[0]
    cc = cv - jnp.mean(cv, axis=-1, keepdims=True)
    var = jnp.mean(cc * cc, axis=-1, keepdims=True)
    ln = cc * lax.rsqrt(var + EPS) * lng_ref[0] + lnb_ref[0]
    y_c = _silu(ln) * _silu(proj(_CZ))
    m = m + gate(1) * _dot(y_c.astype(bf16), wbr_ref[1])

    y_s = proj(_SB) * _short_conv_tile(ys_ref, prev_ref, next_ref, csw_ref) * _silu(proj(_SZ))
    m = m + gate(2) * _dot(y_s.astype(bf16), wbr_ref[2])

    mb = m.astype(bf16)
    for r in range(0, TM_MERGE, TM_MERGE // 2):
        rows = slice(r, r + TM_MERGE // 2)
        out = x[rows] + _dot(mb[rows], wo_ref[...])
        if final:
            out = out * lax.rsqrt(jnp.mean(out * out, axis=-1, keepdims=True) + EPS) * fg_ref[...]
        o_ref[0, rows, :] = out


def _resident(shape, *index):
    return pl.BlockSpec(shape, lambda b, j: index, pipeline_mode=pl.Buffered(1))


def _layer_row(width, l):
    return _resident((1, 1, width), l, 0, 0)


_HBM = pl.BlockSpec(memory_space=pl.ANY)
_STAGE_SCRATCH = [pltpu.VMEM((STAGE_SLOTS, STAGE_ROWS, BRANCH_W), f32),
                  pltpu.SemaphoreType.DMA((STAGE_SLOTS,))]


def _params():
    return pltpu.CompilerParams(dimension_semantics=("arbitrary", "arbitrary"),
                                vmem_limit_bytes=VMEM_LIMIT)


def _tile(width, rows=TM):
    return pl.BlockSpec((1, rows, width), lambda b, j: (b, j, 0))


_SLAB_SHAPE = jax.ShapeDtypeStruct((BATCH, N_SLAB, SEQ, LANES), f32)


_EDGE_SHAPE = jax.ShapeDtypeStruct((BATCH, SEQ // TM_PROJ, 2, SHORT_PAD, BRANCH_W), f32)


def _call_a(l, x, norm_g, w_in, b_in, csg, csw, csb):
    return pl.pallas_call(
        functools.partial(_kernel_a, layer=l),
        grid=(BATCH, SEQ // TM_PROJ),
        in_specs=[_tile(D_MODEL, TM_PROJ), _layer_row(D_MODEL, l), _HBM,
                  _layer_row(IN_W, l), _resident((PAIR_W, 2 * PAIR_W), 0, 0),
                  _resident((1, SHORT_K, BRANCH_W), l, 0, 0), _layer_row(BRANCH_W, l)],
        out_specs=[_tile(D_MODEL, TM_PROJ),
                   pl.BlockSpec((1, 2, 2, TM_PROJ // 2, BRANCH_W), lambda b, j: (b, 0, 0, j, 0)),
                   pl.BlockSpec((1, N_SLAB, TM_PROJ, LANES), lambda b, j: (b, 0, j, 0)),
                   _tile(BRANCH_W, TM_PROJ),
                   pl.BlockSpec((1, 1, 2, SHORT_PAD, BRANCH_W), lambda b, j: (b, j, 0, 0, 0))],
        out_shape=[jax.ShapeDtypeStruct((BATCH, SEQ, D_MODEL), bf16),
                   jax.ShapeDtypeStruct((BATCH, 2, 2, HALF, BRANCH_W), bf16),
                   _SLAB_SHAPE,
                   jax.ShapeDtypeStruct((BATCH, SEQ, BRANCH_W), f32),
                   _EDGE_SHAPE],
        scratch_shapes=[pltpu.VMEM((2 * PAIR_W // LANES, TM_PROJ, LANES), f32),
                        pltpu.VMEM((N_SLAB, TM_PROJ + 2 * SHORT_PAD, LANES), f32),
                        pltpu.VMEM((D_MODEL, len(A_BLOCKS) * BRANCH_W), bf16)] + _STAGE_SCRATCH,
        compiler_params=_params(),
        name="proj_mix_inputs",
    )(x, norm_g, w_in, b_in, csg, csw, csb)


def _halo_specs(pad, half):
    per_tile = TM // pad
    last = SEQ // pad - 1
    shape = (1, N_SLAB, pad, LANES)
    first_tile = half * NT_HALF
    return [pl.BlockSpec(shape, lambda b, j: (
                b, 0, jnp.maximum((j + first_tile) * per_tile - 1, 0), 0)),
            pl.BlockSpec((1, N_SLAB, TM, LANES), lambda b, j: (b, 0, j + first_tile, 0)),
            pl.BlockSpec(shape, lambda b, j: (
                b, 0, jnp.minimum((j + first_tile + 1) * per_tile, last), 0))]


def _call_f(l, tab, xx, c, ccw, ccb):
    halves = pl.BlockSpec((1, 2, TM, BRANCH_W), lambda b, j: (b, 0, j, 0))
    shape = jax.ShapeDtypeStruct((BATCH, 2, HALF, BRANCH_W), f32)
    return pl.pallas_call(
        _kernel_f,
        grid=(BATCH, NT_HALF),
        in_specs=[_resident((2, HALF, SEQ), 0, 0, 0),
                  pl.BlockSpec((1, 2, SEQ, BRANCH_W), lambda b, j: (b, 0, 0, 0))]
                 + _halo_specs(CONF_PAD, 0) + _halo_specs(CONF_PAD, 1) + [
                  _resident((1, CONF_K, BRANCH_W), l, 0, 0), _layer_row(BRANCH_W, l)],
        out_specs=[halves] * 2,
        out_shape=[shape] * 2,
        scratch_shapes=[pltpu.VMEM((2, N_SLAB, TM + 2 * CONF_PAD + CONV_ROWS, LANES), f32),
                        pltpu.VMEM((2, TM, BRANCH_W), f32)],
        compiler_params=_params(),
        name="seq_mix",
    )(tab, xx, *([c] * 6), ccw, ccb)


def _call_b(l, x, h, y, cv, ys, edges, csw, w_in, b_in, lng, lnb, wbr, wo, fg):
    assert TM_MERGE == TM_PROJ
    last = SEQ // TM_MERGE - 1
    edge_shape = (1, 1, 2, SHORT_PAD, BRANCH_W)
    return pl.pallas_call(
        functools.partial(_kernel_b, layer=l, final=(l == DEPTH - 1)),
        grid=(BATCH, SEQ // TM_MERGE),
        in_specs=[_tile(D_MODEL, TM_MERGE), _tile(D_MODEL, TM_MERGE)]
                 + [_tile(BRANCH_W, TM_MERGE)] * 3
                 + [pl.BlockSpec(edge_shape, lambda b, j: (b, jnp.maximum(j - 1, 0), 0, 0, 0)),
                    pl.BlockSpec(edge_shape, lambda b, j: (b, jnp.minimum(j + 1, last), 0, 0, 0)),
                    _resident((1, SHORT_K, BRANCH_W), l, 0, 0),
                    _HBM, _HBM, _HBM,
                    _layer_row(IN_W, l), _layer_row(BRANCH_W, l), _layer_row(BRANCH_W, l),
                    _resident((1, D_MODEL), 0, 0)],
        out_specs=_tile(D_MODEL, TM_MERGE),
        out_shape=jax.ShapeDtypeStruct((BATCH, SEQ, D_MODEL), f32),
        scratch_shapes=[pltpu.VMEM((D_MODEL, len(B_BLOCKS) * BRANCH_W), bf16),
                        pltpu.VMEM((N_BRANCH, BRANCH_W, D_MODEL), bf16),
                        pltpu.VMEM((D_MODEL, D_MODEL), bf16)] + _STAGE_SCRATCH,
        compiler_params=_params(),
        name="merge_out",
    )(x, h, y, cv, ys, edges, edges, csw, w_in, wbr, wo, b_in, lng, lnb, fg)


def kernel(x, norm_g, w_in, b_in, conv_c_w, conv_c_b, ln_c_g, ln_c_b,
           conv_s_w, conv_s_b, w_branch, w_out, final_g):
    tab = jnp.asarray(_SEQ_TABLE).astype(bf16)
    csg = jnp.asarray(_CSG_TABLE).astype(bf16)
    rows = lambda v: v.reshape(DEPTH, 1, -1)
    norm_g, b_in, conv_c_b, ln_c_g, ln_c_b, conv_s_b = map(
        rows, (norm_g, b_in, conv_c_b, ln_c_g, ln_c_b, conv_s_b))
    final_g = final_g.reshape(1, -1)
    for l in range(DEPTH):
        h, xx, c, ys, edges = _call_a(l, x, norm_g, w_in, b_in, csg, conv_s_w, conv_s_b)
        xx = xx.reshape(BATCH, 2, SEQ, BRANCH_W)
        y, cv = (v.reshape(BATCH, SEQ, BRANCH_W) for v in
                 _call_f(l, tab, xx, c, conv_c_w, conv_c_b))
        x = _call_b(l, x, h, y, cv, ys, edges, conv_s_w, w_in, b_in, ln_c_g, ln_c_b,
                    w_branch, w_out, final_g)
    return x
```

```python
import functools

import numpy as np
import jax
import jax.numpy as jnp
from jax import lax
from jax.experimental import pallas as pl
from jax.experimental.pallas import tpu as pltpu

D_MODEL = 1024
BATCH = 8
SEQ = 2048
DEPTH = 2
BRANCH_W = D_MODEL // 2
N_BRANCH = 3
N_GROUPS = 4
GROUP_W = BRANCH_W // N_GROUPS
PAIR_W = 2 * GROUP_W
CONF_K = 31
SHORT_K = 3
EPS = 1e-6
IN_W = 9 * BRANCH_W + N_BRANCH * D_MODEL

TM = 512
NT = SEQ // TM
TM_PROJ = 1024
TM_MERGE = 1024
LANES = 128
N_SLAB = BRANCH_W // LANES
HALF = SEQ // 2
NT_HALF = NT // 2
CONV_ROWS = 32
STAGE_ROWS = BRANCH_W
STAGE_SLOTS = 4
CONF_PAD = 16
SHORT_PAD = 8
VMEM_LIMIT = 63 * 1024 * 1024

_FX, _FZ, _CA, _CB, _CZ, _SB, _SC, _SH, _SZ = range(9)
_GATE0 = 9
A_BLOCKS = (_FX, _CA, _CB, _SC, _SH)
B_BLOCKS = (_FZ, _CZ, _SB, _SZ) + tuple(range(_GATE0, _GATE0 + 2 * N_BRANCH))

bf16 = jnp.bfloat16
f32 = jnp.float32


def _dft_tables():
    k = np.arange(HALF, dtype=np.int64)[:, None]
    m = np.arange(HALF, dtype=np.int64)[None, :]
    seq = []
    for parity in range(2):
        ang = 2.0 * np.pi * (((2 * m + parity) * k) % SEQ).astype(np.float64) / SEQ
        seq.append(np.concatenate([np.cos(ang), -np.sin(ang)], axis=1))
    seq = np.stack(seq)
    c = np.arange(GROUP_W, dtype=np.int64)
    angg = 2.0 * np.pi * ((c[:, None] * c[None, :]) % GROUP_W).astype(np.float64) / GROUP_W
    scale = 1.0 / np.sqrt(float(SEQ * GROUP_W))
    pair = np.eye(PAIR_W // GROUP_W)
    cg = np.kron(pair, np.cos(angg)) * scale
    sg = np.kron(pair, np.sin(angg)) * scale
    return seq.astype(np.float32), np.concatenate([cg, sg], axis=1).astype(np.float32)


_SEQ_TABLE, _CSG_TABLE = _dft_tables()


def _dot(a, b):
    return jnp.dot(a, b, preferred_element_type=f32)


def _sigmoid(v):
    return 0.5 * jnp.tanh(0.5 * v) + 0.5


def _silu(v):
    return v * _sigmoid(v)


def _rms_bf16(x, g):
    y = x * lax.rsqrt(jnp.mean(x * x, axis=-1, keepdims=True) + EPS)
    return (y * g).astype(bf16)


def _store_slabs(ref, v):
    for s in range(N_SLAB):
        ref[0, s] = v[:, s * LANES:(s + 1) * LANES]


def _first_step():
    return (pl.program_id(0) == 0) & (pl.program_id(1) == 0)


def _stage_weights(pairs, stage_ref, sem_ref):
    def copy(i):
        slot = i % STAGE_SLOTS
        return pltpu.make_async_copy(pairs[i][0], stage_ref.at[slot], sem_ref.at[slot])

    for i in range(min(STAGE_SLOTS - 1, len(pairs))):
        copy(i).start()
    for i, (_, dst) in enumerate(pairs):
        if i + STAGE_SLOTS - 1 < len(pairs):
            copy(i + STAGE_SLOTS - 1).start()
        copy(i).wait()
        dst[...] = stage_ref[i % STAGE_SLOTS].astype(bf16)


def _w_in_pairs(w_hbm, l, blocks, wq_ref):
    return [(w_hbm.at[l, pl.ds(r, STAGE_ROWS), pl.ds(blk * BRANCH_W, BRANCH_W)],
             wq_ref.at[pl.ds(r, STAGE_ROWS), pl.ds(i * BRANCH_W, BRANCH_W)])
            for i, blk in enumerate(blocks) for r in range(0, D_MODEL, STAGE_ROWS)]


def _project(h, wq_ref, b_ref, blocks, block, n=1):
    i = blocks.index(block)
    assert blocks[i:i + n] == tuple(range(block, block + n))
    return (_dot(h, wq_ref[:, i * BRANCH_W:(i + n) * BRANCH_W])
            + b_ref[0, :, block * BRANCH_W:(block + n) * BRANCH_W])


def _kernel_a(x_ref, g_ref, w_hbm, b_ref, csg_ref, csw_ref, csb_ref,
              h_ref, xx_ref, c_ref, ys_ref, edge_ref,
              split_ref, uwin_ref, wq_ref, stage_ref, sem_ref, *, layer):
    @pl.when(_first_step())
    def _():
        _stage_weights(_w_in_pairs(w_hbm, layer, A_BLOCKS, wq_ref), stage_ref, sem_ref)

    h = _rms_bf16(x_ref[0], g_ref[0])
    h_ref[0] = h

    proj = functools.partial(_project, h, wq_ref, b_ref, A_BLOCKS)

    u = proj(_SC) * proj(_SH)
    edge_ref[0, 0, 0] = u[:SHORT_PAD]
    edge_ref[0, 0, 1] = u[TM_PROJ - SHORT_PAD:]
    zeros = jnp.zeros((SHORT_PAD, LANES), f32)
    slabs = []
    for s in range(N_SLAB):
        ls = slice(s * LANES, (s + 1) * LANES)
        uwin_ref[s, 0:SHORT_PAD, :] = zeros
        uwin_ref[s, SHORT_PAD:SHORT_PAD + TM_PROJ, :] = u[:, ls]
        uwin_ref[s, SHORT_PAD + TM_PROJ:, :] = zeros
        acc = jnp.broadcast_to(csb_ref[0, :, ls], (TM_PROJ, LANES))
        for k in range(SHORT_K):
            shift = SHORT_PAD - (SHORT_K - 1) // 2 + k
            acc = acc + csw_ref[0, k:k + 1, ls] * uwin_ref[s, shift:shift + TM_PROJ, :]
        slabs.append(acc)
    ys_ref[0] = jnp.concatenate(slabs, axis=1)

    fx = proj(_FX).astype(bf16)
    slabs_per_plane = PAIR_W // LANES
    for p in range(BRANCH_W // PAIR_W):
        cols = slice(p * PAIR_W, (p + 1) * PAIR_W)
        xx = _dot(fx[:, cols], csg_ref[...])
        for q in range(2 * slabs_per_plane):
            split_ref[q] = xx[:, q * LANES:(q + 1) * LANES]
        for parity in range(2):
            rows = [split_ref[q, pl.ds(parity, TM_PROJ // 2, stride=2), :]
                    for q in range(2 * slabs_per_plane)]
            for plane in range(2):
                part = rows[plane * slabs_per_plane:(plane + 1) * slabs_per_plane]
                xx_ref[0, parity, plane, :, cols] = jnp.concatenate(part, axis=1).astype(bf16)
    _store_slabs(c_ref, proj(_CA) * _sigmoid(proj(_CB)))


def _fill_window(win_ref, top_ref, mid_ref, bot_ref, pad, j):
    for s in range(N_SLAB):
        win_ref[s, 0:pad, :] = jnp.where(j > 0, top_ref[0, s], 0.0)
        win_ref[s, pad:pad + TM, :] = mid_ref[0, s]
        win_ref[s, pad + TM:2 * pad + TM, :] = jnp.where(j < NT - 1, bot_ref[0, s], 0.0)


def _depthwise(wins_ref, half, r0, rows, w_ref, b_ref, taps, pad):
    off = pad - (taps - 1) // 2
    park = TM + 2 * pad
    accs = []
    for s in range(N_SLAB):
        ls = slice(s * LANES, (s + 1) * LANES)
        acc = jnp.broadcast_to(b_ref[0, :, ls], (rows, LANES))
        for k in range(taps):
            acc = acc + w_ref[0, k:k + 1, ls] * wins_ref[half, s, pl.ds(r0 + off + k, rows), :]
        wins_ref[0, s, park:park + rows, :] = acc
        accs.append(wins_ref[0, s, park:park + rows, :])
    return jnp.concatenate(accs, axis=1)


def _kernel_f(tab_ref, xx_ref, c0_top, c0_mid, c0_bot, c1_top, c1_mid, c1_bot,
              ccw_ref, ccb_ref, y_ref, cv_ref, winc_ref, eo_ref):
    j = pl.program_id(1)
    for half, (ct, cm, cb) in enumerate(((c0_top, c0_mid, c0_bot), (c1_top, c1_mid, c1_bot))):
        _fill_window(winc_ref.at[half], ct, cm, cb, CONF_PAD, j + half * NT_HALF)
    k_rows = pl.ds(pl.multiple_of(j * TM, TM), TM)

    def body(half, carry):
        eo_ref[half] = _dot(tab_ref[half, k_rows, :], xx_ref[0, half])
        for r0 in range(0, TM, CONV_ROWS):
            cv_ref[0, half, r0:r0 + CONV_ROWS, :] = _depthwise(
                winc_ref, half, r0, CONV_ROWS, ccw_ref, ccb_ref, CONF_K, CONF_PAD)
        return carry

    lax.fori_loop(0, jnp.minimum(j + 2, 2), body, 0)
    even, odd = eo_ref[0], eo_ref[1]
    y_ref[0, 0] = even + odd
    y_ref[0, 1] = even - odd


def _short_conv_tile(ys_ref, prev_ref, next_ref, csw_ref):
    j, last = pl.program_id(1), SEQ // TM_MERGE - 1
    row = lax.broadcasted_iota(jnp.int32, (SHORT_PAD, 1), 0)
    before = csw_ref[0, 0:1, :] * prev_ref[0, 0, 1, SHORT_PAD - 1:SHORT_PAD, :]
    after = csw_ref[0, SHORT_K - 1:SHORT_K, :] * next_ref[0, 0, 0, 0:1, :]
    ys = ys_ref[0]
    top = ys[:SHORT_PAD] + jnp.where((row == 0) & (j > 0), before, 0.0)
    bot = ys[TM_MERGE - SHORT_PAD:] + jnp.where((row == SHORT_PAD - 1) & (j < last), after, 0.0)
    return jnp.concatenate([top, ys[SHORT_PAD:TM_MERGE - SHORT_PAD], bot], axis=0)


def _kernel_b(x_ref, h_ref, y_ref, cv_ref, ys_ref, prev_ref, next_ref, csw_ref,
              w_hbm, wbr_hbm, wo_hbm, b_ref, lng_ref, lnb_ref, fg_ref, o_ref,
              wq_ref, wbr_ref, wo_ref, stage_ref, sem_ref, *, layer, final):
    @pl.when(_first_step())
    def _():
        halves = range(0, D_MODEL, BRANCH_W)
        pairs = _w_in_pairs(w_hbm, layer, B_BLOCKS, wq_ref)
        pairs += [(wbr_hbm.at[layer, k, :, pl.ds(c, BRANCH_W)], wbr_ref.at[k, :, pl.ds(c, BRANCH_W)])
                  for k in range(N_BRANCH) for c in halves]
        pairs += [(wo_hbm.at[layer, pl.ds(r, STAGE_ROWS), pl.ds(c, BRANCH_W)],
                   wo_ref.at[pl.ds(r, STAGE_ROWS), pl.ds(c, BRANCH_W)])
                  for r in range(0, D_MODEL, STAGE_ROWS) for c in halves]
        _stage_weights(pairs, stage_ref, sem_ref)

    x = x_ref[0]

    def proj(block, n=1):
        return _project(h_ref[0], wq_ref, b_ref, B_BLOCKS, block, n)

    def gate(k):
        return _sigmoid(proj(_GATE0 + 2 * k, 2))

    y_f = y_ref[0] * _silu(proj(_FZ))
    m = gate(0) * _dot(y_f.astype(bf16), wbr_ref[0])

    cv = cv_ref[0]
    cc = cv - jnp.mean(cv, axis=-1, keepdims=True)
    var = jnp.mean(cc * cc, axis=-1, keepdims=True)
    ln = cc * lax.rsqrt(var + EPS) * lng_ref[0] + lnb_ref[0]
    y_c = _silu(ln) * _silu(proj(_CZ))
    m = m + gate(1) * _dot(y_c.astype(bf16), wbr_ref[1])

    y_s = proj(_SB) * _short_conv_tile(ys_ref, prev_ref, next_ref, csw_ref) * _silu(proj(_SZ))
    m = m + gate(2) * _dot(y_s.astype(bf16), wbr_ref[2])

    mb = m.astype(bf16)
    for r in range(0, TM_MERGE, TM_MERGE // 2):
        rows = slice(r, r + TM_MERGE // 2)
        out = x[rows] + _dot(mb[rows], wo_ref[...])
        if final:
            out = out * lax.rsqrt(jnp.mean(out * out, axis=-1, keepdims=True) + EPS) * fg_ref[...]
        o_ref[0, rows, :] = out


def _resident(shape, *index):
    return pl.BlockSpec(shape, lambda b, j: index, pipeline_mode=pl.Buffered(1))


def _layer_row(width, l):
    return _resident((1, 1, width), l, 0, 0)


_HBM = pl.BlockSpec(memory_space=pl.ANY)
_STAGE_SCRATCH = [pltpu.VMEM((STAGE_SLOTS, STAGE_ROWS, BRANCH_W), f32),
                  pltpu.SemaphoreType.DMA((STAGE_SLOTS,))]


def _params():
    return pltpu.CompilerParams(dimension_semantics=("arbitrary", "arbitrary"),
                                vmem_limit_bytes=VMEM_LIMIT)


def _tile(width, rows=TM):
    return pl.BlockSpec((1, rows, width), lambda b, j: (b, j, 0))


_SLAB_SHAPE = jax.ShapeDtypeStruct((BATCH, N_SLAB, SEQ, LANES), f32)


_EDGE_SHAPE = jax.ShapeDtypeStruct((BATCH, SEQ // TM_PROJ, 2, SHORT_PAD, BRANCH_W), f32)


def _call_a(l, x, norm_g, w_in, b_in, csg, csw, csb):
    return pl.pallas_call(
        functools.partial(_kernel_a, layer=l),
        grid=(BATCH, SEQ // TM_PROJ),
        in_specs=[_tile(D_MODEL, TM_PROJ), _layer_row(D_MODEL, l), _HBM,
                  _layer_row(IN_W, l), _resident((PAIR_W, 2 * PAIR_W), 0, 0),
                  _resident((1, SHORT_K, BRANCH_W), l, 0, 0), _layer_row(BRANCH_W, l)],
        out_specs=[_tile(D_MODEL, TM_PROJ),
                   pl.BlockSpec((1, 2, 2, TM_PROJ // 2, BRANCH_W), lambda b, j: (b, 0, 0, j, 0)),
                   pl.BlockSpec((1, N_SLAB, TM_PROJ, LANES), lambda b, j: (b, 0, j, 0)),
                   _tile(BRANCH_W, TM_PROJ),
                   pl.BlockSpec((1, 1, 2, SHORT_PAD, BRANCH_W), lambda b, j: (b, j, 0, 0, 0))],
        out_shape=[jax.ShapeDtypeStruct((BATCH, SEQ, D_MODEL), bf16),
                   jax.ShapeDtypeStruct((BATCH, 2, 2, HALF, BRANCH_W), bf16),
                   _SLAB_SHAPE,
                   jax.ShapeDtypeStruct((BATCH, SEQ, BRANCH_W), f32),
                   _EDGE_SHAPE],
        scratch_shapes=[pltpu.VMEM((2 * PAIR_W // LANES, TM_PROJ, LANES), f32),
                        pltpu.VMEM((N_SLAB, TM_PROJ + 2 * SHORT_PAD, LANES), f32),
                        pltpu.VMEM((D_MODEL, len(A_BLOCKS) * BRANCH_W), bf16)] + _STAGE_SCRATCH,
        compiler_params=_params(),
        name="proj_mix_inputs",
    )(x, norm_g, w_in, b_in, csg, csw, csb)


def _halo_specs(pad, half):
    per_tile = TM // pad
    last = SEQ // pad - 1
    shape = (1, N_SLAB, pad, LANES)
    first_tile = half * NT_HALF
    return [pl.BlockSpec(shape, lambda b, j: (
                b, 0, jnp.maximum((j + first_tile) * per_tile - 1, 0), 0)),
            pl.BlockSpec((1, N_SLAB, TM, LANES), lambda b, j: (b, 0, j + first_tile, 0)),
            pl.BlockSpec(shape, lambda b, j: (
                b, 0, jnp.minimum((j + first_tile + 1) * per_tile, last), 0))]


def _call_f(l, tab, xx, c, ccw, ccb):
    halves = pl.BlockSpec((1, 2, TM, BRANCH_W), lambda b, j: (b, 0, j, 0))
    shape = jax.ShapeDtypeStruct((BATCH, 2, HALF, BRANCH_W), f32)
    return pl.pallas_call(
        _kernel_f,
        grid=(BATCH, NT_HALF),
        in_specs=[_resident((2, HALF, SEQ), 0, 0, 0),
                  pl.BlockSpec((1, 2, SEQ, BRANCH_W), lambda b, j: (b, 0, 0, 0))]
                 + _halo_specs(CONF_PAD, 0) + _halo_specs(CONF_PAD, 1) + [
                  _resident((1, CONF_K, BRANCH_W), l, 0, 0), _layer_row(BRANCH_W, l)],
        out_specs=[halves] * 2,
        out_shape=[shape] * 2,
        scratch_shapes=[pltpu.VMEM((2, N_SLAB, TM + 2 * CONF_PAD + CONV_ROWS, LANES), f32),
                        pltpu.VMEM((2, TM, BRANCH_W), f32)],
        compiler_params=_params(),
        name="seq_mix",
    )(tab, xx, *([c] * 6), ccw, ccb)


def _call_b(l, x, h, y, cv, ys, edges, csw, w_in, b_in, lng, lnb, wbr, wo, fg):
    assert TM_MERGE == TM_PROJ
    last = SEQ // TM_MERGE - 1
    edge_shape = (1, 1, 2, SHORT_PAD, BRANCH_W)
    return pl.pallas_call(
        functools.partial(_kernel_b, layer=l, final=(l == DEPTH - 1)),
        grid=(BATCH, SEQ // TM_MERGE),
        in_specs=[_tile(D_MODEL, TM_MERGE), _tile(D_MODEL, TM_MERGE)]
                 + [_tile(BRANCH_W, TM_MERGE)] * 3
                 + [pl.BlockSpec(edge_shape, lambda b, j: (b, jnp.maximum(j - 1, 0), 0, 0, 0)),
                    pl.BlockSpec(edge_shape, lambda b, j: (b, jnp.minimum(j + 1, last), 0, 0, 0)),
                    _resident((1, SHORT_K, BRANCH_W), l, 0, 0),
                    _HBM, _HBM, _HBM,
                    _layer_row(IN_W, l), _layer_row(BRANCH_W, l), _layer_row(BRANCH_W, l),
                    _resident((1, D_MODEL), 0, 0)],
        out_specs=_tile(D_MODEL, TM_MERGE),
        out_shape=jax.ShapeDtypeStruct((BATCH, SEQ, D_MODEL), f32),
        scratch_shapes=[pltpu.VMEM((D_MODEL, len(B_BLOCKS) * BRANCH_W), bf16),
                        pltpu.VMEM((N_BRANCH, BRANCH_W, D_MODEL), bf16),
                        pltpu.VMEM((D_MODEL, D_MODEL), bf16)] + _STAGE_SCRATCH,
        compiler_params=_params(),
        name="merge_out",
    )(x, h, y, cv, ys, edges, edges, csw, w_in, wbr, wo, b_in, lng, lnb, fg)


def kernel(x, norm_g, w_in, b_in, conv_c_w, conv_c_b, ln_c_g, ln_c_b,
           conv_s_w, conv_s_b, w_branch, w_out, final_g):
    tab = jnp.asarray(_SEQ_TABLE).astype(bf16)
    csg = jnp.asarray(_CSG_TABLE).astype(bf16)
    rows = lambda v: v.reshape(DEPTH, 1, -1)
    norm_g, b_in, conv_c_b, ln_c_g, ln_c_b, conv_s_b = map(
        rows, (norm_g, b_in, conv_c_b, ln_c_g, ln_c_b, conv_s_b))
    final_g = final_g.reshape(1, -1)
    for l in range(DEPTH):
        h, xx, c, ys, edges = _call_a(l, x, norm_g, w_in, b_in, csg, conv_s_w, conv_s_b)
        xx = xx.reshape(BATCH, 2, SEQ, BRANCH_W)
        y, cv = (v.reshape(BATCH, SEQ, BRANCH_W) for v in
                 _call_f(l, tab, xx, c, conv_c_w, conv_c_b))
        x = _call_b(l, x, h, y, cv, ys, edges, conv_s_w, w_in, b_in, ln_c_g, ln_c_b,
                    w_branch, w_out, final_g)
    return x
```

```python
import functools

import numpy as np
import jax
import jax.numpy as jnp
from jax import lax
from jax.experimental import pallas as pl
from jax.experimental.pallas import tpu as pltpu

D_MODEL = 1024
BATCH = 8
SEQ = 2048
DEPTH = 2
BRANCH_W = D_MODEL // 2
N_BRANCH = 3
N_GROUPS = 4
GROUP_W = BRANCH_W // N_GROUPS
PAIR_W = 2 * GROUP_W
CONF_K = 31
SHORT_K = 3
EPS = 1e-6
IN_W = 9 * BRANCH_W + N_BRANCH * D_MODEL

TM = 512
NT = SEQ // TM
TM_PROJ = 1024
TM_MERGE = 1024
LANES = 128
N_SLAB = BRANCH_W // LANES
HALF = SEQ // 2
NT_HALF = NT // 2
CONV_ROWS = 40
STAGE_ROWS = BRANCH_W
STAGE_SLOTS = 4
CONF_PAD = 16
SHORT_PAD = 8
VMEM_LIMIT = 63 * 1024 * 1024

_FX, _FZ, _CA, _CB, _CZ, _SB, _SC, _SH, _SZ = range(9)
_GATE0 = 9
A_BLOCKS = (_FX, _CA, _CB, _SC, _SH)
B_BLOCKS = (_FZ, _CZ, _SB, _SZ) + tuple(range(_GATE0, _GATE0 + 2 * N_BRANCH))

bf16 = jnp.bfloat16
f32 = jnp.float32


def _dft_tables():
    k = np.arange(HALF, dtype=np.int64)[:, None]
    m = np.arange(HALF, dtype=np.int64)[None, :]
    seq = []
    for parity in range(2):
        ang = 2.0 * np.pi * (((2 * m + parity) * k) % SEQ).astype(np.float64) / SEQ
        seq.append(np.concatenate([np.cos(ang), -np.sin(ang)], axis=1))
    seq = np.stack(seq)
    c = np.arange(GROUP_W, dtype=np.int64)
    angg = 2.0 * np.pi * ((c[:, None] * c[None, :]) % GROUP_W).astype(np.float64) / GROUP_W
    scale = 1.0 / np.sqrt(float(SEQ * GROUP_W))
    pair = np.eye(PAIR_W // GROUP_W)
    cg = np.kron(pair, np.cos(angg)) * scale
    sg = np.kron(pair, np.sin(angg)) * scale
    return seq.astype(np.float32), np.concatenate([cg, sg], axis=1).astype(np.float32)


_SEQ_TABLE, _CSG_TABLE = _dft_tables()


def _dot(a, b):
    return jnp.dot(a, b, preferred_element_type=f32)


def _silu(v):
    return v * jax.nn.sigmoid(v)


def _rms_bf16(x, g):
    y = x * lax.rsqrt(jnp.mean(x * x, axis=-1, keepdims=True) + EPS)
    return (y * g).astype(bf16)


def _store_slabs(ref, v):
    for s in range(N_SLAB):
        ref[0, s] = v[:, s * LANES:(s + 1) * LANES]


def _first_step():
    return (pl.program_id(0) == 0) & (pl.program_id(1) == 0)


def _stage_weights(pairs, stage_ref, sem_ref):
    def copy(i):
        slot = i % STAGE_SLOTS
        return pltpu.make_async_copy(pairs[i][0], stage_ref.at[slot], sem_ref.at[slot])

    for i in range(min(STAGE_SLOTS - 1, len(pairs))):
        copy(i).start()
    for i, (_, dst) in enumerate(pairs):
        if i + STAGE_SLOTS - 1 < len(pairs):
            copy(i + STAGE_SLOTS - 1).start()
        copy(i).wait()
        dst[...] = stage_ref[i % STAGE_SLOTS].astype(bf16)


def _w_in_pairs(w_hbm, l, blocks, wq_ref):
    return [(w_hbm.at[l, pl.ds(r, STAGE_ROWS), pl.ds(blk * BRANCH_W, BRANCH_W)],
             wq_ref.at[pl.ds(r, STAGE_ROWS), pl.ds(i * BRANCH_W, BRANCH_W)])
            for i, blk in enumerate(blocks) for r in range(0, D_MODEL, STAGE_ROWS)]


def _project(h, wq_ref, b_ref, blocks, block, n=1):
    i = blocks.index(block)
    assert blocks[i:i + n] == tuple(range(block, block + n))
    return (_dot(h, wq_ref[:, i * BRANCH_W:(i + n) * BRANCH_W])
            + b_ref[0, :, block * BRANCH_W:(block + n) * BRANCH_W])


def _kernel_a(x_ref, g_ref, w_hbm, b_ref, csg_ref, csw_ref, csb_ref,
              h_ref, xx_ref, c_ref, ys_ref, edge_ref,
              split_ref, uwin_ref, wq_ref, stage_ref, sem_ref, *, layer):
    @pl.when(_first_step())
    def _():
        _stage_weights(_w_in_pairs(w_hbm, layer, A_BLOCKS, wq_ref), stage_ref, sem_ref)

    h = _rms_bf16(x_ref[0], g_ref[0])
    h_ref[0] = h

    proj = functools.partial(_project, h, wq_ref, b_ref, A_BLOCKS)

    fx = proj(_FX).astype(bf16)
    slabs_per_plane = PAIR_W // LANES
    for p in range(BRANCH_W // PAIR_W):
        cols = slice(p * PAIR_W, (p + 1) * PAIR_W)
        xx = _dot(fx[:, cols], csg_ref[...])
        for q in range(2 * slabs_per_plane):
            split_ref[q] = xx[:, q * LANES:(q + 1) * LANES]
        for parity in range(2):
            rows = [split_ref[q, pl.ds(parity, TM_PROJ // 2, stride=2), :]
                    for q in range(2 * slabs_per_plane)]
            for plane in range(2):
                part = rows[plane * slabs_per_plane:(plane + 1) * slabs_per_plane]
                xx_ref[0, parity, plane, :, cols] = jnp.concatenate(part, axis=1).astype(bf16)
    _store_slabs(c_ref, proj(_CA) * jax.nn.sigmoid(proj(_CB)))

    u = proj(_SC) * proj(_SH)
    edge_ref[0, 0, 0] = u[:SHORT_PAD]
    edge_ref[0, 0, 1] = u[TM_PROJ - SHORT_PAD:]
    zeros = jnp.zeros((SHORT_PAD, LANES), f32)
    slabs = []
    for s in range(N_SLAB):
        ls = slice(s * LANES, (s + 1) * LANES)
        uwin_ref[s, 0:SHORT_PAD, :] = zeros
        uwin_ref[s, SHORT_PAD:SHORT_PAD + TM_PROJ, :] = u[:, ls]
        uwin_ref[s, SHORT_PAD + TM_PROJ:, :] = zeros
        acc = jnp.broadcast_to(csb_ref[0, :, ls], (TM_PROJ, LANES))
        for k in range(SHORT_K):
            shift = SHORT_PAD - (SHORT_K - 1) // 2 + k
            acc = acc + csw_ref[0, k:k + 1, ls] * uwin_ref[s, shift:shift + TM_PROJ, :]
        slabs.append(acc)
    ys_ref[0] = jnp.concatenate(slabs, axis=1)


def _fill_window(win_ref, top_ref, mid_ref, bot_ref, pad, j):
    for s in range(N_SLAB):
        win_ref[s, 0:pad, :] = jnp.where(j > 0, top_ref[0, s], 0.0)
        win_ref[s, pad:pad + TM, :] = mid_ref[0, s]
        win_ref[s, pad + TM:2 * pad + TM, :] = jnp.where(j < NT - 1, bot_ref[0, s], 0.0)


def _depthwise(wins_ref, half, r0, rows, w_ref, b_ref, taps, pad):
    off = pad - (taps - 1) // 2
    park = TM + 2 * pad
    accs = []
    for s in range(N_SLAB):
        ls = slice(s * LANES, (s + 1) * LANES)
        acc = jnp.broadcast_to(b_ref[0, :, ls], (rows, LANES))
        for k in range(taps):
            acc = acc + w_ref[0, k:k + 1, ls] * wins_ref[half, s, pl.ds(r0 + off + k, rows), :]
        wins_ref[0, s, park:park + rows, :] = acc
        accs.append(wins_ref[0, s, park:park + rows, :])
    return jnp.concatenate(accs, axis=1)


def _kernel_f(tab_ref, xx_ref, c0_top, c0_mid, c0_bot, c1_top, c1_mid, c1_bot,
              ccw_ref, ccb_ref, y_ref, cv_ref, winc_ref, eo_ref):
    j = pl.program_id(1)
    for half, (ct, cm, cb) in enumerate(((c0_top, c0_mid, c0_bot), (c1_top, c1_mid, c1_bot))):
        _fill_window(winc_ref.at[half], ct, cm, cb, CONF_PAD, j + half * NT_HALF)
    k_rows = pl.ds(pl.multiple_of(j * TM, TM), TM)

    def body(half, carry):
        eo_ref[half] = _dot(tab_ref[half, k_rows, :], xx_ref[0, half])
        for r0 in range(0, TM, CONV_ROWS):
            rows = min(CONV_ROWS, TM - r0)
            cv_ref[0, half, r0:r0 + rows, :] = _depthwise(
                winc_ref, half, r0, rows, ccw_ref, ccb_ref, CONF_K, CONF_PAD)
        return carry

    lax.fori_loop(0, jnp.minimum(j + 2, 2), body, 0)
    even, odd = eo_ref[0], eo_ref[1]
    y_ref[0, 0] = even + odd
    y_ref[0, 1] = even - odd


def _short_conv_tile(ys_ref, prev_ref, next_ref, csw_ref):
    j, last = pl.program_id(1), SEQ // TM_MERGE - 1
    row = lax.broadcasted_iota(jnp.int32, (SHORT_PAD, 1), 0)
    before = csw_ref[0, 0:1, :] * prev_ref[0, 0, 1, SHORT_PAD - 1:SHORT_PAD, :]
    after = csw_ref[0, SHORT_K - 1:SHORT_K, :] * next_ref[0, 0, 0, 0:1, :]
    ys = ys_ref[0]
    top = ys[:SHORT_PAD] + jnp.where((row == 0) & (j > 0), before, 0.0)
    bot = ys[TM_MERGE - SHORT_PAD:] + jnp.where((row == SHORT_PAD - 1) & (j < last), after, 0.0)
    return jnp.concatenate([top, ys[SHORT_PAD:TM_MERGE - SHORT_PAD], bot], axis=0)


def _kernel_b(x_ref, h_ref, y_ref, cv_ref, ys_ref, prev_ref, next_ref, csw_ref,
              w_hbm, wbr_hbm, wo_hbm, b_ref, lng_ref, lnb_ref, fg_ref, o_ref,
              wq_ref, wbr_ref, wo_ref, stage_ref, sem_ref, *, layer, final):
    @pl.when(_first_step())
    def _():
        halves = range(0, D_MODEL, BRANCH_W)
        pairs = _w_in_pairs(w_hbm, layer, B_BLOCKS, wq_ref)
        pairs += [(wbr_hbm.at[layer, k, :, pl.ds(c, BRANCH_W)], wbr_ref.at[k, :, pl.ds(c, BRANCH_W)])
                  for k in range(N_BRANCH) for c in halves]
        pairs += [(wo_hbm.at[layer, pl.ds(r, STAGE_ROWS), pl.ds(c, BRANCH_W)],
                   wo_ref.at[pl.ds(r, STAGE_ROWS), pl.ds(c, BRANCH_W)])
                  for r in range(0, D_MODEL, STAGE_ROWS) for c in halves]
        _stage_weights(pairs, stage_ref, sem_ref)

    x = x_ref[0]

    def proj(block, n=1):
        return _project(h_ref[0], wq_ref, b_ref, B_BLOCKS, block, n)

    def gate(k):
        return jax.nn.sigmoid(proj(_GATE0 + 2 * k, 2))

    y_f = y_ref[0] * _silu(proj(_FZ))
    m = gate(0) * _dot(y_f.astype(bf16), wbr_ref[0])

    cv = cv_ref[0]
    cc = cv - jnp.mean(cv, axis=-1, keepdims=True)
    var = jnp.mean(cc * cc, axis=-1, keepdims=True)
    ln = cc * lax.rsqrt(var + EPS) * lng_ref[0] + lnb_ref[0]
    y_c = _silu(ln) * _silu(proj(_CZ))
    m = m + gate(1) * _dot(y_c.astype(bf16), wbr_ref[1])

    y_s = proj(_SB) * _short_conv_tile(ys_ref, prev_ref, next_ref, csw_ref) * _silu(proj(_SZ))
    m = m + gate(2) * _dot(y_s.astype(bf16), wbr_ref[2])

    mb = m.astype(bf16)
    for r in range(0, TM_MERGE, TM_MERGE // 2):
        rows = slice(r, r + TM_MERGE // 2)
        out = x[rows] + _dot(mb[rows], wo_ref[...])
        if final:
            out = out * lax.rsqrt(jnp.mean(out * out, axis=-1, keepdims=True) + EPS) * fg_ref[...]
        o_ref[0, rows, :] = out


def _resident(shape, *index):
    return pl.BlockSpec(shape, lambda b, j: index, pipeline_mode=pl.Buffered(1))


def _layer_row(width, l):
    return _resident((1, 1, width), l, 0, 0)


_HBM = pl.BlockSpec(memory_space=pl.ANY)
_STAGE_SCRATCH = [pltpu.VMEM((STAGE_SLOTS, STAGE_ROWS, BRANCH_W), f32),
                  pltpu.SemaphoreType.DMA((STAGE_SLOTS,))]


def _params():
    return pltpu.CompilerParams(dimension_semantics=("arbitrary", "arbitrary"),
                                vmem_limit_bytes=VMEM_LIMIT)


def _tile(width, rows=TM):
    return pl.BlockSpec((1, rows, width), lambda b, j: (b, j, 0))


_SLAB_SHAPE = jax.ShapeDtypeStruct((BATCH, N_SLAB, SEQ, LANES), f32)


_EDGE_SHAPE = jax.ShapeDtypeStruct((BATCH, SEQ // TM_PROJ, 2, SHORT_PAD, BRANCH_W), f32)


def _call_a(l, x, norm_g, w_in, b_in, csg, csw, csb):
    return pl.pallas_call(
        functools.partial(_kernel_a, layer=l),
        grid=(BATCH, SEQ // TM_PROJ),
        in_specs=[_tile(D_MODEL, TM_PROJ), _layer_row(D_MODEL, l), _HBM,
                  _layer_row(IN_W, l), _resident((PAIR_W, 2 * PAIR_W), 0, 0),
                  _resident((1, SHORT_K, BRANCH_W), l, 0, 0), _layer_row(BRANCH_W, l)],
        out_specs=[_tile(D_MODEL, TM_PROJ),
                   pl.BlockSpec((1, 2, 2, TM_PROJ // 2, BRANCH_W), lambda b, j: (b, 0, 0, j, 0)),
                   pl.BlockSpec((1, N_SLAB, TM_PROJ, LANES), lambda b, j: (b, 0, j, 0)),
                   _tile(BRANCH_W, TM_PROJ),
                   pl.BlockSpec((1, 1, 2, SHORT_PAD, BRANCH_W), lambda b, j: (b, j, 0, 0, 0))],
        out_shape=[jax.ShapeDtypeStruct((BATCH, SEQ, D_MODEL), bf16),
                   jax.ShapeDtypeStruct((BATCH, 2, 2, HALF, BRANCH_W), bf16),
                   _SLAB_SHAPE,
                   jax.ShapeDtypeStruct((BATCH, SEQ, BRANCH_W), f32),
                   _EDGE_SHAPE],
        scratch_shapes=[pltpu.VMEM((2 * PAIR_W // LANES, TM_PROJ, LANES), f32),
                        pltpu.VMEM((N_SLAB, TM_PROJ + 2 * SHORT_PAD, LANES), f32),
                        pltpu.VMEM((D_MODEL, len(A_BLOCKS) * BRANCH_W), bf16)] + _STAGE_SCRATCH,
        compiler_params=_params(),
        name="proj_mix_inputs",
    )(x, norm_g, w_in, b_in, csg, csw, csb)


def _halo_specs(pad, half):
    per_tile = TM // pad
    last = SEQ // pad - 1
    shape = (1, N_SLAB, pad, LANES)
    first_tile = half * NT_HALF
    return [pl.BlockSpec(shape, lambda b, j: (
                b, 0, jnp.maximum((j + first_tile) * per_tile - 1, 0), 0)),
            pl.BlockSpec((1, N_SLAB, TM, LANES), lambda b, j: (b, 0, j + first_tile, 0)),
            pl.BlockSpec(shape, lambda b, j: (
                b, 0, jnp.minimum((j + first_tile + 1) * per_tile, last), 0))]


def _call_f(l, tab, xx, c, ccw, ccb):
    halves = pl.BlockSpec((1, 2, TM, BRANCH_W), lambda b, j: (b, 0, j, 0))
    shape = jax.ShapeDtypeStruct((BATCH, 2, HALF, BRANCH_W), f32)
    return pl.pallas_call(
        _kernel_f,
        grid=(BATCH, NT_HALF),
        in_specs=[_resident((2, HALF, SEQ), 0, 0, 0),
                  pl.BlockSpec((1, 2, SEQ, BRANCH_W), lambda b, j: (b, 0, 0, 0))]
                 + _halo_specs(CONF_PAD, 0) + _halo_specs(CONF_PAD, 1) + [
                  _resident((1, CONF_K, BRANCH_W), l, 0, 0), _layer_row(BRANCH_W, l)],
        out_specs=[halves] * 2,
        out_shape=[shape] * 2,
        scratch_shapes=[pltpu.VMEM((2, N_SLAB, TM + 2 * CONF_PAD + CONV_ROWS, LANES), f32),
                        pltpu.VMEM((2, TM, BRANCH_W), f32)],
        compiler_params=_params(),
        name="seq_mix",
    )(tab, xx, *([c] * 6), ccw, ccb)


def _call_b(l, x, h, y, cv, ys, edges, csw, w_in, b_in, lng, lnb, wbr, wo, fg):
    assert TM_MERGE == TM_PROJ
    last = SEQ // TM_MERGE - 1
    edge_shape = (1, 1, 2, SHORT_PAD, BRANCH_W)
    return pl.pallas_call(
        functools.partial(_kernel_b, layer=l, final=(l == DEPTH - 1)),
        grid=(BATCH, SEQ // TM_MERGE),
        in_specs=[_tile(D_MODEL, TM_MERGE), _tile(D_MODEL, TM_MERGE)]
                 + [_tile(BRANCH_W, TM_MERGE)] * 3
                 + [pl.BlockSpec(edge_shape, lambda b, j: (b, jnp.maximum(j - 1, 0), 0, 0, 0)),
                    pl.BlockSpec(edge_shape, lambda b, j: (b, jnp.minimum(j + 1, last), 0, 0, 0)),
                    _resident((1, SHORT_K, BRANCH_W), l, 0, 0),
                    _HBM, _HBM, _HBM,
                    _layer_row(IN_W, l), _layer_row(BRANCH_W, l), _layer_row(BRANCH_W, l),
                    _resident((1, D_MODEL), 0, 0)],
        out_specs=_tile(D_MODEL, TM_MERGE),
        out_shape=jax.ShapeDtypeStruct((BATCH, SEQ, D_MODEL), f32),
        scratch_shapes=[pltpu.VMEM((D_MODEL, len(B_BLOCKS) * BRANCH_W), bf16),
                        pltpu.VMEM((N_BRANCH, BRANCH_W, D_MODEL), bf16),
                        pltpu.VMEM((D_MODEL, D_MODEL), bf16)] + _STAGE_SCRATCH,
        compiler_params=_params(),
        name="merge_out",
    )(x, h, y, cv, ys, edges, edges, csw, w_in, wbr, wo, b_in, lng, lnb, fg)


def kernel(x, norm_g, w_in, b_in, conv_c_w, conv_c_b, ln_c_g, ln_c_b,
           conv_s_w, conv_s_b, w_branch, w_out, final_g):
    tab = jnp.asarray(_SEQ_TABLE).astype(bf16)
    csg = jnp.asarray(_CSG_TABLE).astype(bf16)
    rows = lambda v: v.reshape(DEPTH, 1, -1)
    norm_g, b_in, conv_c_b, ln_c_g, ln_c_b, conv_s_b = map(
        rows, (norm_g, b_in, conv_c_b, ln_c_g, ln_c_b, conv_s_b))
    final_g = final_g.reshape(1, -1)
    for l in range(DEPTH):
        h, xx, c, ys, edges = _call_a(l, x, norm_g, w_in, b_in, csg, conv_s_w, conv_s_b)
        xx = xx.reshape(BATCH, 2, SEQ, BRANCH_W)
        y, cv = (v.reshape(BATCH, SEQ, BRANCH_W) for v in
                 _call_f(l, tab, xx, c, conv_c_w, conv_c_b))
        x = _call_b(l, x, h, y, cv, ys, edges, conv_s_w, w_in, b_in, ln_c_g, ln_c_b,
                    w_branch, w_out, final_g)
    return x
```

```python
import functools

import numpy as np
import jax
import jax.numpy as jnp
from jax import lax
from jax.experimental import pallas as pl
from jax.experimental.pallas import tpu as pltpu

D_MODEL = 1024
BATCH = 8
SEQ = 2048
DEPTH = 2
BRANCH_W = D_MODEL // 2
N_BRANCH = 3
N_GROUPS = 4
GROUP_W = BRANCH_W // N_GROUPS
PAIR_W = 2 * GROUP_W
CONF_K = 31
SHORT_K = 3
EPS = 1e-6
IN_W = 9 * BRANCH_W + N_BRANCH * D_MODEL

TM = 512
NT = SEQ // TM
TM_PROJ = 1024
N_PROJ = SEQ // TM_PROJ
TM_MERGE = 1024
LANES = 128
N_SLAB = BRANCH_W // LANES
HALF = SEQ // 2
NT_HALF = NT // 2
CONV_ROWS = 40
STAGE_ROWS = BRANCH_W
STAGE_SLOTS = 4
CONF_PAD = 16
SHORT_PAD = 8
VMEM_LIMIT = 63 * 1024 * 1024

_FX, _FZ, _CA, _CB, _CZ, _SB, _SC, _SH, _SZ = range(9)
_GATE0 = 9
A_BLOCKS = (_FX, _CA, _CB, _SC, _SH)
B_BLOCKS = (_FZ, _CZ, _SB, _SZ) + tuple(range(_GATE0, _GATE0 + 2 * N_BRANCH))

bf16 = jnp.bfloat16
f32 = jnp.float32


def _dft_tables():
    k = np.arange(HALF, dtype=np.int64)[:, None]
    m = np.arange(HALF, dtype=np.int64)[None, :]
    seq = []
    for parity in range(2):
        ang = 2.0 * np.pi * (((2 * m + parity) * k) % SEQ).astype(np.float64) / SEQ
        seq.append(np.concatenate([np.cos(ang), -np.sin(ang)], axis=1))
    seq = np.stack(seq)
    c = np.arange(GROUP_W, dtype=np.int64)
    angg = 2.0 * np.pi * ((c[:, None] * c[None, :]) % GROUP_W).astype(np.float64) / GROUP_W
    scale = 1.0 / np.sqrt(float(SEQ * GROUP_W))
    pair = np.eye(PAIR_W // GROUP_W)
    cg = np.kron(pair, np.cos(angg)) * scale
    sg = np.kron(pair, np.sin(angg)) * scale
    return seq.astype(np.float32), np.concatenate([cg, sg], axis=1).astype(np.float32)


_SEQ_TABLE, _CSG_TABLE = _dft_tables()


def _dot(a, b):
    return jnp.dot(a, b, preferred_element_type=f32)


def _silu(v):
    return v * jax.nn.sigmoid(v)


def _rms_bf16(x, g):
    y = x * lax.rsqrt(jnp.mean(x * x, axis=-1, keepdims=True) + EPS)
    return (y * g).astype(bf16)


def _first_step():
    return (pl.program_id(0) == 0) & (pl.program_id(1) == 0)


def _stage_weights(pairs, stage_ref, sem_ref):
    def copy(i):
        slot = i % STAGE_SLOTS
        return pltpu.make_async_copy(pairs[i][0], stage_ref.at[slot], sem_ref.at[slot])

    for i in range(min(STAGE_SLOTS - 1, len(pairs))):
        copy(i).start()
    for i, (_, dst) in enumerate(pairs):
        if i + STAGE_SLOTS - 1 < len(pairs):
            copy(i + STAGE_SLOTS - 1).start()
        copy(i).wait()
        dst[...] = stage_ref[i % STAGE_SLOTS].astype(bf16)


def _w_in_pairs(w_hbm, l, blocks, wq_ref):
    return [(w_hbm.at[l, pl.ds(r, STAGE_ROWS), pl.ds(blk * BRANCH_W, BRANCH_W)],
             wq_ref.at[pl.ds(r, STAGE_ROWS), pl.ds(i * BRANCH_W, BRANCH_W)])
            for i, blk in enumerate(blocks) for r in range(0, D_MODEL, STAGE_ROWS)]


def _project(h, wq_ref, b_ref, blocks, block, n=1):
    i = blocks.index(block)
    assert blocks[i:i + n] == tuple(range(block, block + n))
    return (_dot(h, wq_ref[:, i * BRANCH_W:(i + n) * BRANCH_W])
            + b_ref[0, :, block * BRANCH_W:(block + n) * BRANCH_W])


def _kernel_af(x_ref, g_ref, w_hbm, b_ref, csg_ref, csw_ref, csb_ref, tab_ref, ccw_ref, ccb_ref,
               h_ref, ys_ref, edge_ref, y_ref, cv_ref,
               split_ref, uwin_ref, wq_ref, stage_ref, sem_ref, xx_ref, cwin_ref, eo_ref,
               *, layer):
    p = pl.program_id(1)

    @pl.when(_first_step())
    def _():
        _stage_weights(_w_in_pairs(w_hbm, layer, A_BLOCKS, wq_ref), stage_ref, sem_ref)

    @pl.when(p < N_PROJ)
    def _():
        _proj_phase(p, x_ref, g_ref, b_ref, csg_ref, csw_ref, csb_ref, h_ref, ys_ref, edge_ref,
                    split_ref, uwin_ref, wq_ref, xx_ref, cwin_ref)

    @pl.when(p >= N_PROJ)
    def _():
        _seq_phase(p - N_PROJ, tab_ref, ccw_ref, ccb_ref, y_ref, cv_ref, xx_ref, cwin_ref, eo_ref)


def _proj_phase(p, x_ref, g_ref, b_ref, csg_ref, csw_ref, csb_ref, h_ref, ys_ref, edge_ref,
                split_ref, uwin_ref, wq_ref, xx_ref, cwin_ref):
    h = _rms_bf16(x_ref[0], g_ref[0])
    h_ref[0] = h

    proj = functools.partial(_project, h, wq_ref, b_ref, A_BLOCKS)

    fx = proj(_FX).astype(bf16)
    slabs_per_plane = PAIR_W // LANES
    for pair in range(BRANCH_W // PAIR_W):
        cols = slice(pair * PAIR_W, (pair + 1) * PAIR_W)
        xx = _dot(fx[:, cols], csg_ref[...])
        for q in range(2 * slabs_per_plane):
            split_ref[q] = xx[:, q * LANES:(q + 1) * LANES]
        for parity in range(2):
            rows = [split_ref[q, pl.ds(parity, TM_PROJ // 2, stride=2), :]
                    for q in range(2 * slabs_per_plane)]
            for plane in range(2):
                part = rows[plane * slabs_per_plane:(plane + 1) * slabs_per_plane]
                at = pl.multiple_of(plane * HALF + p * (TM_PROJ // 2), TM_PROJ // 2)
                xx_ref[parity, pl.ds(at, TM_PROJ // 2), cols] = (
                    jnp.concatenate(part, axis=1).astype(bf16))

    glu = proj(_CA) * jax.nn.sigmoid(proj(_CB))
    at = pl.multiple_of(CONF_PAD + p * TM_PROJ, CONF_PAD)
    for s in range(N_SLAB):
        cwin_ref[s, 0:CONF_PAD, :] = jnp.zeros((CONF_PAD, LANES), f32)
        cwin_ref[s, CONF_PAD + SEQ:2 * CONF_PAD + SEQ, :] = jnp.zeros((CONF_PAD, LANES), f32)
        cwin_ref[s, pl.ds(at, TM_PROJ), :] = glu[:, s * LANES:(s + 1) * LANES]

    u = proj(_SC) * proj(_SH)
    edge_ref[0, 0, 0] = u[:SHORT_PAD]
    edge_ref[0, 0, 1] = u[TM_PROJ - SHORT_PAD:]
    zeros = jnp.zeros((SHORT_PAD, LANES), f32)
    slabs = []
    for s in range(N_SLAB):
        ls = slice(s * LANES, (s + 1) * LANES)
        uwin_ref[s, 0:SHORT_PAD, :] = zeros
        uwin_ref[s, SHORT_PAD:SHORT_PAD + TM_PROJ, :] = u[:, ls]
        uwin_ref[s, SHORT_PAD + TM_PROJ:, :] = zeros
        acc = jnp.broadcast_to(csb_ref[0, :, ls], (TM_PROJ, LANES))
        for k in range(SHORT_K):
            shift = SHORT_PAD - (SHORT_K - 1) // 2 + k
            acc = acc + csw_ref[0, k:k + 1, ls] * uwin_ref[s, shift:shift + TM_PROJ, :]
        slabs.append(acc)
    ys_ref[0] = jnp.concatenate(slabs, axis=1)


def _depthwise(win_ref, base, r0, rows, w_ref, b_ref):
    off = CONF_PAD - (CONF_K - 1) // 2
    park = SEQ + 2 * CONF_PAD
    accs = []
    for s in range(N_SLAB):
        ls = slice(s * LANES, (s + 1) * LANES)
        acc = jnp.broadcast_to(b_ref[0, :, ls], (rows, LANES))
        for k in range(CONF_K):
            acc = acc + w_ref[0, k:k + 1, ls] * win_ref[s, pl.ds(base + r0 + off + k, rows), :]
        win_ref[s, park:park + rows, :] = acc
        accs.append(win_ref[s, park:park + rows, :])
    return jnp.concatenate(accs, axis=1)


def _seq_phase(j, tab_ref, ccw_ref, ccb_ref, y_ref, cv_ref, xx_ref, cwin_ref, eo_ref):
    k_rows = pl.ds(pl.multiple_of(j * TM, TM), TM)

    def body(half, carry):
        eo_ref[half] = _dot(tab_ref[half, k_rows, :], xx_ref[half])
        base = pl.multiple_of((j + half * NT_HALF) * TM, TM)
        for r0 in range(0, TM, CONV_ROWS):
            rows = min(CONV_ROWS, TM - r0)
            cv_ref[0, half, r0:r0 + rows, :] = _depthwise(
                cwin_ref, base, r0, rows, ccw_ref, ccb_ref)
        return carry

    lax.fori_loop(0, jnp.minimum(j + 2, 2), body, 0)
    even, odd = eo_ref[0], eo_ref[1]
    y_ref[0, 0] = even + odd
    y_ref[0, 1] = even - odd


def _short_conv_tile(ys_ref, prev_ref, next_ref, csw_ref):
    j, last = pl.program_id(1), SEQ // TM_MERGE - 1
    row = lax.broadcasted_iota(jnp.int32, (SHORT_PAD, 1), 0)
    before = csw_ref[0, 0:1, :] * prev_ref[0, 0, 1, SHORT_PAD - 1:SHORT_PAD, :]
    after = csw_ref[0, SHORT_K - 1:SHORT_K, :] * next_ref[0, 0, 0, 0:1, :]
    ys = ys_ref[0]
    top = ys[:SHORT_PAD] + jnp.where((row == 0) & (j > 0), before, 0.0)
    bot = ys[TM_MERGE - SHORT_PAD:] + jnp.where((row == SHORT_PAD - 1) & (j < last), after, 0.0)
    return jnp.concatenate([top, ys[SHORT_PAD:TM_MERGE - SHORT_PAD], bot], axis=0)


def _kernel_b(x_ref, h_ref, y_ref, cv_ref, ys_ref, prev_ref, next_ref, csw_ref,
              w_hbm, wbr_hbm, wo_hbm, b_ref, lng_ref, lnb_ref, fg_ref, o_ref,
              wq_ref, wbr_ref, wo_ref, stage_ref, sem_ref, *, layer, final):
    @pl.when(_first_step())
    def _():
        halves = range(0, D_MODEL, BRANCH_W)
        pairs = _w_in_pairs(w_hbm, layer, B_BLOCKS, wq_ref)
        pairs += [(wbr_hbm.at[layer, k, :, pl.ds(c, BRANCH_W)], wbr_ref.at[k, :, pl.ds(c, BRANCH_W)])
                  for k in range(N_BRANCH) for c in halves]
        pairs += [(wo_hbm.at[layer, pl.ds(r, STAGE_ROWS), pl.ds(c, BRANCH_W)],
                   wo_ref.at[pl.ds(r, STAGE_ROWS), pl.ds(c, BRANCH_W)])
                  for r in range(0, D_MODEL, STAGE_ROWS) for c in halves]
        _stage_weights(pairs, stage_ref, sem_ref)

    x = x_ref[0]

    def proj(block, n=1):
        return _project(h_ref[0], wq_ref, b_ref, B_BLOCKS, block, n)

    def gate(k):
        return jax.nn.sigmoid(proj(_GATE0 + 2 * k, 2))

    y_f = y_ref[0] * _silu(proj(_FZ))
    m = gate(0) * _dot(y_f.astype(bf16), wbr_ref[0])

    cv = cv_ref[0]
    cc = cv - jnp.mean(cv, axis=-1, keepdims=True)
    var = jnp.mean(cc * cc, axis=-1, keepdims=True)
    ln = cc * lax.rsqrt(var + EPS) * lng_ref[0] + lnb_ref[0]
    y_c = _silu(ln) * _silu(proj(_CZ))
    m = m + gate(1) * _dot(y_c.astype(bf16), wbr_ref[1])

    y_s = proj(_SB) * _short_conv_tile(ys_ref, prev_ref, next_ref, csw_ref) * _silu(proj(_SZ))
    m = m + gate(2) * _dot(y_s.astype(bf16), wbr_ref[2])

    mb = m.astype(bf16)
    for r in range(0, TM_MERGE, TM_MERGE // 2):
        rows = slice(r, r + TM_MERGE // 2)
        out = x[rows] + _dot(mb[rows], wo_ref[...])
        if final:
            out = out * lax.rsqrt(jnp.mean(out * out, axis=-1, keepdims=True) + EPS) * fg_ref[...]
        o_ref[0, rows, :] = out


def _resident(shape, *index):
    return pl.BlockSpec(shape, lambda b, j: index, pipeline_mode=pl.Buffered(1))


def _layer_row(width, l):
    return _resident((1, 1, width), l, 0, 0)


_HBM = pl.BlockSpec(memory_space=pl.ANY)
_STAGE_SCRATCH = [pltpu.VMEM((STAGE_SLOTS, STAGE_ROWS, BRANCH_W), f32),
                  pltpu.SemaphoreType.DMA((STAGE_SLOTS,))]


def _params():
    return pltpu.CompilerParams(dimension_semantics=("arbitrary", "arbitrary"),
                                vmem_limit_bytes=VMEM_LIMIT)


def _tile(width, rows=TM):
    return pl.BlockSpec((1, rows, width), lambda b, j: (b, j, 0))


_SLAB_SHAPE = jax.ShapeDtypeStruct((BATCH, N_SLAB, SEQ, LANES), f32)


_EDGE_SHAPE = jax.ShapeDtypeStruct((BATCH, SEQ // TM_PROJ, 2, SHORT_PAD, BRANCH_W), f32)


def _call_af(l, x, norm_g, w_in, b_in, csg, csw, csb, tab, ccw, ccb):
    proj_step = lambda p: jnp.minimum(p, N_PROJ - 1)
    seq_step = lambda p: jnp.maximum(p - N_PROJ, 0)
    halves = pl.BlockSpec((1, 2, TM, BRANCH_W), lambda b, p: (b, 0, seq_step(p), 0))
    half_shape = jax.ShapeDtypeStruct((BATCH, 2, HALF, BRANCH_W), f32)
    return pl.pallas_call(
        functools.partial(_kernel_af, layer=l),
        grid=(BATCH, N_PROJ + NT_HALF),
        in_specs=[pl.BlockSpec((1, TM_PROJ, D_MODEL), lambda b, p: (b, proj_step(p), 0)),
                  _layer_row(D_MODEL, l), _HBM,
                  _layer_row(IN_W, l), _resident((PAIR_W, 2 * PAIR_W), 0, 0),
                  _resident((1, SHORT_K, BRANCH_W), l, 0, 0), _layer_row(BRANCH_W, l),
                  _resident((2, HALF, SEQ), 0, 0, 0),
                  _resident((1, CONF_K, BRANCH_W), l, 0, 0), _layer_row(BRANCH_W, l)],
        out_specs=[pl.BlockSpec((1, TM_PROJ, D_MODEL), lambda b, p: (b, proj_step(p), 0)),
                   pl.BlockSpec((1, TM_PROJ, BRANCH_W), lambda b, p: (b, proj_step(p), 0)),
                   pl.BlockSpec((1, 1, 2, SHORT_PAD, BRANCH_W),
                                lambda b, p: (b, proj_step(p), 0, 0, 0)),
                   halves, halves],
        out_shape=[jax.ShapeDtypeStruct((BATCH, SEQ, D_MODEL), bf16),
                   jax.ShapeDtypeStruct((BATCH, SEQ, BRANCH_W), f32),
                   _EDGE_SHAPE, half_shape, half_shape],
        scratch_shapes=[pltpu.VMEM((2 * PAIR_W // LANES, TM_PROJ, LANES), f32),
                        pltpu.VMEM((N_SLAB, TM_PROJ + 2 * SHORT_PAD, LANES), f32),
                        pltpu.VMEM((D_MODEL, len(A_BLOCKS) * BRANCH_W), bf16)] + _STAGE_SCRATCH
                       + [pltpu.VMEM((2, SEQ, BRANCH_W), bf16),
                          pltpu.VMEM((N_SLAB, SEQ + 2 * CONF_PAD + CONV_ROWS, LANES), f32),
                          pltpu.VMEM((2, TM, BRANCH_W), f32)],
        compiler_params=_params(),
        name="proj_seq_mix",
    )(x, norm_g, w_in, b_in, csg, csw, csb, tab, ccw, ccb)


def _call_b(l, x, h, y, cv, ys, edges, csw, w_in, b_in, lng, lnb, wbr, wo, fg):
    assert TM_MERGE == TM_PROJ
    last = SEQ // TM_MERGE - 1
    edge_shape = (1, 1, 2, SHORT_PAD, BRANCH_W)
    return pl.pallas_call(
        functools.partial(_kernel_b, layer=l, final=(l == DEPTH - 1)),
        grid=(BATCH, SEQ // TM_MERGE),
        in_specs=[_tile(D_MODEL, TM_MERGE), _tile(D_MODEL, TM_MERGE)]
                 + [_tile(BRANCH_W, TM_MERGE)] * 3
                 + [pl.BlockSpec(edge_shape, lambda b, j: (b, jnp.maximum(j - 1, 0), 0, 0, 0)),
                    pl.BlockSpec(edge_shape, lambda b, j: (b, jnp.minimum(j + 1, last), 0, 0, 0)),
                    _resident((1, SHORT_K, BRANCH_W), l, 0, 0),
                    _HBM, _HBM, _HBM,
                    _layer_row(IN_W, l), _layer_row(BRANCH_W, l), _layer_row(BRANCH_W, l),
                    _resident((1, D_MODEL), 0, 0)],
        out_specs=_tile(D_MODEL, TM_MERGE),
        out_shape=jax.ShapeDtypeStruct((BATCH, SEQ, D_MODEL), f32),
        scratch_shapes=[pltpu.VMEM((D_MODEL, len(B_BLOCKS) * BRANCH_W), bf16),
                        pltpu.VMEM((N_BRANCH, BRANCH_W, D_MODEL), bf16),
                        pltpu.VMEM((D_MODEL, D_MODEL), bf16)] + _STAGE_SCRATCH,
        compiler_params=_params(),
        name="merge_out",
    )(x, h, y, cv, ys, edges, edges, csw, w_in, wbr, wo, b_in, lng, lnb, fg)


def kernel(x, norm_g, w_in, b_in, conv_c_w, conv_c_b, ln_c_g, ln_c_b,
           conv_s_w, conv_s_b, w_branch, w_out, final_g):
    tab = jnp.asarray(_SEQ_TABLE).astype(bf16)
    csg = jnp.asarray(_CSG_TABLE).astype(bf16)
    rows = lambda v: v.reshape(DEPTH, 1, -1)
    norm_g, b_in, conv_c_b, ln_c_g, ln_c_b, conv_s_b = map(
        rows, (norm_g, b_in, conv_c_b, ln_c_g, ln_c_b, conv_s_b))
    final_g = final_g.reshape(1, -1)
    for l in range(DEPTH):
        h, ys, edges, y, cv = _call_af(l, x, norm_g, w_in, b_in, csg, conv_s_w, conv_s_b,
                                       tab, conv_c_w, conv_c_b)
        y, cv = y.reshape(BATCH, SEQ, BRANCH_W), cv.reshape(BATCH, SEQ, BRANCH_W)
        x = _call_b(l, x, h, y, cv, ys, edges, conv_s_w, w_in, b_in, ln_c_g, ln_c_b,
                    w_branch, w_out, final_g)
    return x
```

```python
import functools

import numpy as np
import jax
import jax.numpy as jnp
from jax import lax
from jax.experimental import pallas as pl
from jax.experimental.pallas import tpu as pltpu

D_MODEL = 1024
BATCH = 8
SEQ = 2048
DEPTH = 2
BRANCH_W = D_MODEL // 2
N_BRANCH = 3
N_GROUPS = 4
GROUP_W = BRANCH_W // N_GROUPS
PAIR_W = 2 * GROUP_W
CONF_K = 31
SHORT_K = 3
EPS = 1e-6
IN_W = 9 * BRANCH_W + N_BRANCH * D_MODEL

TM = 512
NT = SEQ // TM
TM_PROJ = 1024
N_PROJ = SEQ // TM_PROJ
TM_MERGE = 1024
LANES = 128
N_SLAB = BRANCH_W // LANES
HALF = SEQ // 2
NT_HALF = NT // 2
CONV_ROWS = 40
STAGE_ROWS = BRANCH_W
STAGE_SLOTS = 4
CONV_SPLITS = (960, 1504, SEQ)
CONF_PAD = 16
SHORT_PAD = 8
VMEM_LIMIT = 63 * 1024 * 1024

_FX, _FZ, _CA, _CB, _CZ, _SB, _SC, _SH, _SZ = range(9)
_GATE0 = 9
A_BLOCKS = (_FX, _CA, _CB, _SC, _SH)
B_BLOCKS = (_FZ, _CZ, _SB, _SZ) + tuple(range(_GATE0, _GATE0 + 2 * N_BRANCH))

bf16 = jnp.bfloat16
f32 = jnp.float32


def _dft_tables():
    k = np.arange(HALF, dtype=np.int64)[:, None]
    m = np.arange(HALF, dtype=np.int64)[None, :]
    seq = []
    for parity in range(2):
        ang = 2.0 * np.pi * (((2 * m + parity) * k) % SEQ).astype(np.float64) / SEQ
        seq.append(np.concatenate([np.cos(ang), -np.sin(ang)], axis=1))
    seq = np.stack(seq)
    c = np.arange(GROUP_W, dtype=np.int64)
    angg = 2.0 * np.pi * ((c[:, None] * c[None, :]) % GROUP_W).astype(np.float64) / GROUP_W
    scale = 1.0 / np.sqrt(float(SEQ * GROUP_W))
    pair = np.eye(PAIR_W // GROUP_W)
    cg = np.kron(pair, np.cos(angg)) * scale
    sg = np.kron(pair, np.sin(angg)) * scale
    return seq.astype(np.float32), np.concatenate([cg, sg], axis=1).astype(np.float32)


_SEQ_TABLE, _CSG_TABLE = _dft_tables()


def _dot(a, b):
    return jnp.dot(a, b, preferred_element_type=f32)


def _silu(v):
    return v * jax.nn.sigmoid(v)


def _rms_bf16(x, g):
    y = x * lax.rsqrt(jnp.mean(x * x, axis=-1, keepdims=True) + EPS)
    return (y * g).astype(bf16)


def _first_step():
    return (pl.program_id(0) == 0) & (pl.program_id(1) == 0)


def _stage_weights(pairs, stage_ref, sem_ref):
    slots = stage_ref.shape[0]

    def copy(i):
        return pltpu.make_async_copy(pairs[i][0], stage_ref.at[i % slots], sem_ref.at[i % slots])

    for i in range(min(slots - 1, len(pairs))):
        copy(i).start()
    for i, (_, dst) in enumerate(pairs):
        if i + slots - 1 < len(pairs):
            copy(i + slots - 1).start()
        copy(i).wait()
        dst[...] = stage_ref[i % slots].astype(bf16)


def _w_in_pairs(w_hbm, l, blocks, wq_ref):
    return [(w_hbm.at[l, pl.ds(r, STAGE_ROWS), pl.ds(blk * BRANCH_W, BRANCH_W)],
             wq_ref.at[pl.ds(r, STAGE_ROWS), pl.ds(i * BRANCH_W, BRANCH_W)])
            for i, blk in enumerate(blocks) for r in range(0, D_MODEL, STAGE_ROWS)]


def _project(h, wq_ref, b_ref, blocks, block, n=1):
    i = blocks.index(block)
    assert blocks[i:i + n] == tuple(range(block, block + n))
    return (_dot(h, wq_ref[:, i * BRANCH_W:(i + n) * BRANCH_W])
            + b_ref[0, :, block * BRANCH_W:(block + n) * BRANCH_W])


def _kernel_af(x_ref, g_ref, w_hbm, b_ref, csg_ref, csw_ref, csb_ref, tab_ref, ccw_ref, ccb_ref,
               h_ref, ys_ref, edge_ref, y_ref, cv_ref,
               split_ref, uwin_ref, wq_ref, stage_ref, sem_ref, xx_ref, cwin_ref,
               *, layer):
    assert N_PROJ == 2 and NT_HALF == 2
    p = pl.program_id(1)
    conv = functools.partial(_conv_rows, cwin_ref, ccw_ref, ccb_ref, cv_ref)

    @pl.when(_first_step())
    def _():
        _stage_weights(_w_in_pairs(w_hbm, layer, A_BLOCKS, wq_ref), stage_ref, sem_ref)

    for step in range(N_PROJ):
        @pl.when(p == step)
        def _(step=step):
            if step == N_PROJ - 1:
                conv(p - step, 0, CONV_SPLITS[0])
            _proj_phase(step, x_ref, g_ref, b_ref, csg_ref, csw_ref, csb_ref, h_ref, ys_ref,
                        edge_ref, split_ref, uwin_ref, wq_ref, xx_ref, cwin_ref)

    for j in range(NT_HALF):
        @pl.when(p == N_PROJ + j)
        def _(j=j):
            conv(p - N_PROJ - j, CONV_SPLITS[j], CONV_SPLITS[j + 1])
            _seq_phase(j, tab_ref, y_ref, xx_ref)


def _conv_rows(win_ref, w_ref, b_ref, cv_ref, zero, lo, hi):
    for r0 in range(lo, hi, CONV_ROWS):
        rows = min(CONV_ROWS, hi - r0)
        cv_ref[0, r0:r0 + rows, :] = _depthwise(win_ref, zero, r0, rows, w_ref, b_ref)


def _proj_phase(p, x_ref, g_ref, b_ref, csg_ref, csw_ref, csb_ref, h_ref, ys_ref, edge_ref,
                split_ref, uwin_ref, wq_ref, xx_ref, cwin_ref):
    h = _rms_bf16(x_ref[0], g_ref[0])
    h_ref[0] = h

    proj = functools.partial(_project, h, wq_ref, b_ref, A_BLOCKS)

    fx = proj(_FX).astype(bf16)
    slabs_per_plane = PAIR_W // LANES
    for pair in range(BRANCH_W // PAIR_W):
        cols = slice(pair * PAIR_W, (pair + 1) * PAIR_W)
        xx = _dot(fx[:, cols], csg_ref[...])
        for q in range(2 * slabs_per_plane):
            split_ref[q] = xx[:, q * LANES:(q + 1) * LANES]
        for parity in range(2):
            rows = [split_ref[q, pl.ds(parity, TM_PROJ // 2, stride=2), :]
                    for q in range(2 * slabs_per_plane)]
            for plane in range(2):
                part = rows[plane * slabs_per_plane:(plane + 1) * slabs_per_plane]
                at = plane * HALF + p * (TM_PROJ // 2)
                xx_ref[parity, at:at + TM_PROJ // 2, cols] = (
                    jnp.concatenate(part, axis=1).astype(bf16))

    glu = proj(_CA) * jax.nn.sigmoid(proj(_CB))
    at = CONF_PAD + p * TM_PROJ
    for s in range(N_SLAB):
        cwin_ref[s, 0:CONF_PAD, :] = jnp.zeros((CONF_PAD, LANES), f32)
        cwin_ref[s, CONF_PAD + SEQ:2 * CONF_PAD + SEQ, :] = jnp.zeros((CONF_PAD, LANES), f32)
        cwin_ref[s, at:at + TM_PROJ, :] = glu[:, s * LANES:(s + 1) * LANES]

    u = proj(_SC) * proj(_SH)
    edge_ref[0, 0, 0] = u[:SHORT_PAD]
    edge_ref[0, 0, 1] = u[TM_PROJ - SHORT_PAD:]
    zeros = jnp.zeros((SHORT_PAD, LANES), f32)
    slabs = []
    for s in range(N_SLAB):
        ls = slice(s * LANES, (s + 1) * LANES)
        uwin_ref[s, 0:SHORT_PAD, :] = zeros
        uwin_ref[s, SHORT_PAD:SHORT_PAD + TM_PROJ, :] = u[:, ls]
        uwin_ref[s, SHORT_PAD + TM_PROJ:, :] = zeros
        acc = jnp.broadcast_to(csb_ref[0, :, ls], (TM_PROJ, LANES))
        for k in range(SHORT_K):
            shift = SHORT_PAD - (SHORT_K - 1) // 2 + k
            acc = acc + csw_ref[0, k:k + 1, ls] * uwin_ref[s, shift:shift + TM_PROJ, :]
        slabs.append(acc)
    ys_ref[0] = jnp.concatenate(slabs, axis=1)


def _depthwise(win_ref, base, r0, rows, w_ref, b_ref):
    off = CONF_PAD - (CONF_K - 1) // 2
    park = SEQ + 2 * CONF_PAD
    accs = []
    for s in range(N_SLAB):
        ls = slice(s * LANES, (s + 1) * LANES)
        acc = jnp.broadcast_to(b_ref[0, :, ls], (rows, LANES))
        for k in range(CONF_K):
            acc = acc + w_ref[0, k:k + 1, ls] * win_ref[s, pl.ds(base + r0 + off + k, rows), :]
        win_ref[s, park:park + rows, :] = acc
        accs.append(win_ref[s, park:park + rows, :])
    return jnp.concatenate(accs, axis=1)


def _seq_phase(j, tab_ref, y_ref, xx_ref):
    k_rows = slice(j * TM, (j + 1) * TM)
    even = _dot(tab_ref[0, k_rows, :], xx_ref[0])
    odd = _dot(tab_ref[1, k_rows, :], xx_ref[1])
    y_ref[0, 0] = even + odd
    y_ref[0, 1] = even - odd


def _short_conv_tile(ys_ref, prev_ref, next_ref, csw_ref):
    j, last = pl.program_id(1), SEQ // TM_MERGE - 1
    row = lax.broadcasted_iota(jnp.int32, (SHORT_PAD, 1), 0)
    before = csw_ref[0, 0:1, :] * prev_ref[0, 0, 1, SHORT_PAD - 1:SHORT_PAD, :]
    after = csw_ref[0, SHORT_K - 1:SHORT_K, :] * next_ref[0, 0, 0, 0:1, :]
    ys = ys_ref[0]
    top = ys[:SHORT_PAD] + jnp.where((row == 0) & (j > 0), before, 0.0)
    bot = ys[TM_MERGE - SHORT_PAD:] + jnp.where((row == SHORT_PAD - 1) & (j < last), after, 0.0)
    return jnp.concatenate([top, ys[SHORT_PAD:TM_MERGE - SHORT_PAD], bot], axis=0)


def _kernel_b(x_ref, h_ref, y_ref, cv_ref, ys_ref, prev_ref, next_ref, csw_ref,
              w_hbm, wbr_hbm, wo_hbm, b_ref, lng_ref, lnb_ref, fg_ref, o_ref,
              wq_ref, wbr_ref, wo_ref, stage_ref, sem_ref, *, layer, final):
    @pl.when(_first_step())
    def _():
        halves = range(0, D_MODEL, BRANCH_W)
        pairs = _w_in_pairs(w_hbm, layer, B_BLOCKS, wq_ref)
        pairs += [(wbr_hbm.at[layer, k, :, pl.ds(c, BRANCH_W)], wbr_ref.at[k, :, pl.ds(c, BRANCH_W)])
                  for k in range(N_BRANCH) for c in halves]
        pairs += [(wo_hbm.at[layer, pl.ds(r, STAGE_ROWS), pl.ds(c, BRANCH_W)],
                   wo_ref.at[pl.ds(r, STAGE_ROWS), pl.ds(c, BRANCH_W)])
                  for r in range(0, D_MODEL, STAGE_ROWS) for c in halves]
        _stage_weights(pairs, stage_ref, sem_ref)

    x = x_ref[0]

    def proj(block, n=1):
        return _project(h_ref[0], wq_ref, b_ref, B_BLOCKS, block, n)

    def gate(k):
        return jax.nn.sigmoid(proj(_GATE0 + 2 * k, 2))

    y_f = y_ref[0] * _silu(proj(_FZ))
    m = gate(0) * _dot(y_f.astype(bf16), wbr_ref[0])

    cv = cv_ref[0]
    cc = cv - jnp.mean(cv, axis=-1, keepdims=True)
    var = jnp.mean(cc * cc, axis=-1, keepdims=True)
    ln = cc * lax.rsqrt(var + EPS) * lng_ref[0] + lnb_ref[0]
    y_c = _silu(ln) * _silu(proj(_CZ))
    m = m + gate(1) * _dot(y_c.astype(bf16), wbr_ref[1])

    y_s = proj(_SB) * _short_conv_tile(ys_ref, prev_ref, next_ref, csw_ref) * _silu(proj(_SZ))
    m = m + gate(2) * _dot(y_s.astype(bf16), wbr_ref[2])

    mb = m.astype(bf16)
    for r in range(0, TM_MERGE, TM_MERGE // 2):
        rows = slice(r, r + TM_MERGE // 2)
        out = x[rows] + _dot(mb[rows], wo_ref[...])
        if final:
            out = out * lax.rsqrt(jnp.mean(out * out, axis=-1, keepdims=True) + EPS) * fg_ref[...]
        o_ref[0, rows, :] = out


def _resident(shape, *index):
    return pl.BlockSpec(shape, lambda b, j: index, pipeline_mode=pl.Buffered(1))


def _layer_row(width, l):
    return _resident((1, 1, width), l, 0, 0)


_HBM = pl.BlockSpec(memory_space=pl.ANY)
def _stage_scratch(slots):
    return [pltpu.VMEM((slots, STAGE_ROWS, BRANCH_W), f32), pltpu.SemaphoreType.DMA((slots,))]


def _params():
    return pltpu.CompilerParams(dimension_semantics=("arbitrary", "arbitrary"),
                                vmem_limit_bytes=VMEM_LIMIT)


def _tile(width, rows=TM):
    return pl.BlockSpec((1, rows, width), lambda b, j: (b, j, 0))


_SLAB_SHAPE = jax.ShapeDtypeStruct((BATCH, N_SLAB, SEQ, LANES), f32)


_EDGE_SHAPE = jax.ShapeDtypeStruct((BATCH, SEQ // TM_PROJ, 2, SHORT_PAD, BRANCH_W), f32)


def _call_af(l, x, norm_g, w_in, b_in, csg, csw, csb, tab, ccw, ccb):
    proj_step = lambda p: jnp.minimum(p, N_PROJ - 1)
    seq_step = lambda p: jnp.maximum(p - N_PROJ, 0)
    halves = pl.BlockSpec((1, 2, TM, BRANCH_W), lambda b, p: (b, 0, seq_step(p), 0))
    half_shape = jax.ShapeDtypeStruct((BATCH, 2, HALF, BRANCH_W), f32)
    return pl.pallas_call(
        functools.partial(_kernel_af, layer=l),
        grid=(BATCH, N_PROJ + NT_HALF),
        in_specs=[pl.BlockSpec((1, TM_PROJ, D_MODEL), lambda b, p: (b, proj_step(p), 0)),
                  _layer_row(D_MODEL, l), _HBM,
                  _layer_row(IN_W, l), _resident((PAIR_W, 2 * PAIR_W), 0, 0),
                  _resident((1, SHORT_K, BRANCH_W), l, 0, 0), _layer_row(BRANCH_W, l),
                  _resident((2, HALF, SEQ), 0, 0, 0),
                  _resident((1, CONF_K, BRANCH_W), l, 0, 0), _layer_row(BRANCH_W, l)],
        out_specs=[pl.BlockSpec((1, TM_PROJ, D_MODEL), lambda b, p: (b, proj_step(p), 0)),
                   pl.BlockSpec((1, TM_PROJ, BRANCH_W), lambda b, p: (b, proj_step(p), 0)),
                   pl.BlockSpec((1, 1, 2, SHORT_PAD, BRANCH_W),
                                lambda b, p: (b, proj_step(p), 0, 0, 0)),
                   halves,
                   pl.BlockSpec((1, SEQ, BRANCH_W), lambda b, p: (b, 0, 0))],
        out_shape=[jax.ShapeDtypeStruct((BATCH, SEQ, D_MODEL), bf16),
                   jax.ShapeDtypeStruct((BATCH, SEQ, BRANCH_W), f32),
                   _EDGE_SHAPE, half_shape,
                   jax.ShapeDtypeStruct((BATCH, SEQ, BRANCH_W), f32)],
        scratch_shapes=[pltpu.VMEM((2 * PAIR_W // LANES, TM_PROJ, LANES), f32),
                        pltpu.VMEM((N_SLAB, TM_PROJ + 2 * SHORT_PAD, LANES), f32),
                        pltpu.VMEM((D_MODEL, len(A_BLOCKS) * BRANCH_W), bf16)]
                       + _stage_scratch(2)
                       + [pltpu.VMEM((2, SEQ, BRANCH_W), bf16),
                          pltpu.VMEM((N_SLAB, SEQ + 2 * CONF_PAD + CONV_ROWS, LANES), f32)],
        compiler_params=_params(),
        name="proj_seq_mix",
    )(x, norm_g, w_in, b_in, csg, csw, csb, tab, ccw, ccb)


def _call_b(l, x, h, y, cv, ys, edges, csw, w_in, b_in, lng, lnb, wbr, wo, fg):
    assert TM_MERGE == TM_PROJ
    last = SEQ // TM_MERGE - 1
    edge_shape = (1, 1, 2, SHORT_PAD, BRANCH_W)
    return pl.pallas_call(
        functools.partial(_kernel_b, layer=l, final=(l == DEPTH - 1)),
        grid=(BATCH, SEQ // TM_MERGE),
        in_specs=[_tile(D_MODEL, TM_MERGE), _tile(D_MODEL, TM_MERGE)]
                 + [_tile(BRANCH_W, TM_MERGE)] * 3
                 + [pl.BlockSpec(edge_shape, lambda b, j: (b, jnp.maximum(j - 1, 0), 0, 0, 0)),
                    pl.BlockSpec(edge_shape, lambda b, j: (b, jnp.minimum(j + 1, last), 0, 0, 0)),
                    _resident((1, SHORT_K, BRANCH_W), l, 0, 0),
                    _HBM, _HBM, _HBM,
                    _layer_row(IN_W, l), _layer_row(BRANCH_W, l), _layer_row(BRANCH_W, l),
                    _resident((1, D_MODEL), 0, 0)],
        out_specs=_tile(D_MODEL, TM_MERGE),
        out_shape=jax.ShapeDtypeStruct((BATCH, SEQ, D_MODEL), f32),
        scratch_shapes=[pltpu.VMEM((D_MODEL, len(B_BLOCKS) * BRANCH_W), bf16),
                        pltpu.VMEM((N_BRANCH, BRANCH_W, D_MODEL), bf16),
                        pltpu.VMEM((D_MODEL, D_MODEL), bf16)] + _stage_scratch(STAGE_SLOTS),
        compiler_params=_params(),
        name="merge_out",
    )(x, h, y, cv, ys, edges, edges, csw, w_in, wbr, wo, b_in, lng, lnb, fg)


def kernel(x, norm_g, w_in, b_in, conv_c_w, conv_c_b, ln_c_g, ln_c_b,
           conv_s_w, conv_s_b, w_branch, w_out, final_g):
    tab = jnp.asarray(_SEQ_TABLE).astype(bf16)
    csg = jnp.asarray(_CSG_TABLE).astype(bf16)
    rows = lambda v: v.reshape(DEPTH, 1, -1)
    norm_g, b_in, conv_c_b, ln_c_g, ln_c_b, conv_s_b = map(
        rows, (norm_g, b_in, conv_c_b, ln_c_g, ln_c_b, conv_s_b))
    final_g = final_g.reshape(1, -1)
    for l in range(DEPTH):
        h, ys, edges, y, cv = _call_af(l, x, norm_g, w_in, b_in, csg, conv_s_w, conv_s_b,
                                       tab, conv_c_w, conv_c_b)
        y = y.reshape(BATCH, SEQ, BRANCH_W)
        x = _call_b(l, x, h, y, cv, ys, edges, conv_s_w, w_in, b_in, ln_c_g, ln_c_b,
                    w_branch, w_out, final_g)
    return x
```

```python
import functools

import numpy as np
import jax
import jax.numpy as jnp
from jax import lax
from jax.experimental import pallas as pl
from jax.experimental.pallas import tpu as pltpu

D_MODEL = 1024
BATCH = 8
SEQ = 2048
DEPTH = 2
BRANCH_W = D_MODEL // 2
N_BRANCH = 3
N_GROUPS = 4
GROUP_W = BRANCH_W // N_GROUPS
PAIR_W = 2 * GROUP_W
CONF_K = 31
SHORT_K = 3
EPS = 1e-6
IN_W = 9 * BRANCH_W + N_BRANCH * D_MODEL

TM = 512
NT = SEQ // TM
TM_PROJ = 1024
N_PROJ = SEQ // TM_PROJ
TM_MERGE = 1024
LANES = 128
N_SLAB = BRANCH_W // LANES
HALF = SEQ // 2
NT_HALF = NT // 2
CONV_ROWS = 40
STAGE_ROWS = BRANCH_W
STAGE_SLOTS = 4
CONV_SPLITS = (1000, 1520, SEQ)
CONF_PAD = 16
SHORT_PAD = 8
VMEM_LIMIT = 63 * 1024 * 1024

_FX, _FZ, _CA, _CB, _CZ, _SB, _SC, _SH, _SZ = range(9)
_GATE0 = 9
A_BLOCKS = (_FX, _CA, _CB, _SC, _SH)
B_BLOCKS = (_FZ, _CZ, _SB, _SZ) + tuple(range(_GATE0, _GATE0 + 2 * N_BRANCH))

bf16 = jnp.bfloat16
f32 = jnp.float32


def _dft_tables():
    k = np.arange(HALF, dtype=np.int64)[:, None]
    m = np.arange(HALF, dtype=np.int64)[None, :]
    seq = []
    for parity in range(2):
        ang = 2.0 * np.pi * (((2 * m + parity) * k) % SEQ).astype(np.float64) / SEQ
        seq.append(np.concatenate([np.cos(ang), -np.sin(ang)], axis=1))
    seq = np.stack(seq)
    c = np.arange(GROUP_W, dtype=np.int64)
    angg = 2.0 * np.pi * ((c[:, None] * c[None, :]) % GROUP_W).astype(np.float64) / GROUP_W
    scale = 1.0 / np.sqrt(float(SEQ * GROUP_W))
    pair = np.eye(PAIR_W // GROUP_W)
    cg = np.kron(pair, np.cos(angg)) * scale
    sg = np.kron(pair, np.sin(angg)) * scale
    return seq.astype(np.float32), np.concatenate([cg, sg], axis=1).astype(np.float32)


_SEQ_TABLE, _CSG_TABLE = _dft_tables()


def _dot(a, b):
    return jnp.dot(a, b, preferred_element_type=f32)


def _silu(v):
    return v * jax.nn.sigmoid(v)


def _rms_bf16(x, g):
    y = x * lax.rsqrt(jnp.mean(x * x, axis=-1, keepdims=True) + EPS)
    return (y * g).astype(bf16)


def _first_step():
    return (pl.program_id(0) == 0) & (pl.program_id(1) == 0)


def _stage_weights(pairs, stage_ref, sem_ref):
    slots = stage_ref.shape[0]

    def copy(i):
        return pltpu.make_async_copy(pairs[i][0], stage_ref.at[i % slots], sem_ref.at[i % slots])

    for i in range(min(slots - 1, len(pairs))):
        copy(i).start()
    for i, (_, dst) in enumerate(pairs):
        if i + slots - 1 < len(pairs):
            copy(i + slots - 1).start()
        copy(i).wait()
        dst[...] = stage_ref[i % slots].astype(bf16)


def _w_in_pairs(w_hbm, l, blocks, wq_ref):
    return [(w_hbm.at[l, pl.ds(r, STAGE_ROWS), pl.ds(blk * BRANCH_W, BRANCH_W)],
             wq_ref.at[pl.ds(r, STAGE_ROWS), pl.ds(i * BRANCH_W, BRANCH_W)])
            for i, blk in enumerate(blocks) for r in range(0, D_MODEL, STAGE_ROWS)]


def _project(h, wq_ref, b_ref, blocks, block, n=1):
    i = blocks.index(block)
    assert blocks[i:i + n] == tuple(range(block, block + n))
    return (_dot(h, wq_ref[:, i * BRANCH_W:(i + n) * BRANCH_W])
            + b_ref[0, :, block * BRANCH_W:(block + n) * BRANCH_W])


def _kernel_af(x_ref, g_ref, w_hbm, b_ref, csg_ref, csw_ref, csb_ref, tab_ref, ccw_ref, ccb_ref,
               h_ref, ys_ref, edge_ref, y_ref, cv_ref,
               split_ref, uwin_ref, wq_ref, stage_ref, sem_ref, xx_ref, cwin_ref,
               *, layer):
    assert N_PROJ == 2 and NT_HALF == 2
    p = pl.program_id(1)
    conv = functools.partial(_conv_rows, cwin_ref, ccw_ref, ccb_ref, cv_ref)

    @pl.when(_first_step())
    def _():
        _stage_weights(_w_in_pairs(w_hbm, layer, A_BLOCKS, wq_ref), stage_ref, sem_ref)

    for step in range(N_PROJ):
        @pl.when(p == step)
        def _(step=step):
            if step == N_PROJ - 1:
                conv(p - step, 0, CONV_SPLITS[0])
            _proj_phase(step, x_ref, g_ref, b_ref, csg_ref, csw_ref, csb_ref, h_ref, ys_ref,
                        edge_ref, split_ref, uwin_ref, wq_ref, xx_ref, cwin_ref)

    for j in range(NT_HALF):
        @pl.when(p == N_PROJ + j)
        def _(j=j):
            conv(p - N_PROJ - j, CONV_SPLITS[j], CONV_SPLITS[j + 1])
            _seq_phase(j, tab_ref, y_ref, xx_ref)


def _conv_rows(win_ref, w_ref, b_ref, cv_ref, zero, lo, hi):
    for r0 in range(lo, hi, CONV_ROWS):
        rows = min(CONV_ROWS, hi - r0)
        cv_ref[0, r0:r0 + rows, :] = _depthwise(win_ref, zero, r0, rows, w_ref, b_ref)


def _proj_phase(p, x_ref, g_ref, b_ref, csg_ref, csw_ref, csb_ref, h_ref, ys_ref, edge_ref,
                split_ref, uwin_ref, wq_ref, xx_ref, cwin_ref):
    h = _rms_bf16(x_ref[0], g_ref[0])
    h_ref[0] = h

    proj = functools.partial(_project, h, wq_ref, b_ref, A_BLOCKS)

    fx = proj(_FX).astype(bf16)
    slabs_per_plane = PAIR_W // LANES
    for pair in range(BRANCH_W // PAIR_W):
        cols = slice(pair * PAIR_W, (pair + 1) * PAIR_W)
        xx = _dot(fx[:, cols], csg_ref[...])
        for q in range(2 * slabs_per_plane):
            split_ref[q] = xx[:, q * LANES:(q + 1) * LANES]
        for parity in range(2):
            rows = [split_ref[q, pl.ds(parity, TM_PROJ // 2, stride=2), :]
                    for q in range(2 * slabs_per_plane)]
            for plane in range(2):
                part = rows[plane * slabs_per_plane:(plane + 1) * slabs_per_plane]
                at = plane * HALF + p * (TM_PROJ // 2)
                xx_ref[parity, at:at + TM_PROJ // 2, cols] = (
                    jnp.concatenate(part, axis=1).astype(bf16))

    glu = proj(_CA) * jax.nn.sigmoid(proj(_CB))
    at = CONF_PAD + p * TM_PROJ
    for s in range(N_SLAB):
        cwin_ref[s, 0:CONF_PAD, :] = jnp.zeros((CONF_PAD, LANES), f32)
        cwin_ref[s, CONF_PAD + SEQ:2 * CONF_PAD + SEQ, :] = jnp.zeros((CONF_PAD, LANES), f32)
        cwin_ref[s, at:at + TM_PROJ, :] = glu[:, s * LANES:(s + 1) * LANES]

    u = proj(_SC) * proj(_SH)
    edge_ref[0, 0, 0] = u[:SHORT_PAD]
    edge_ref[0, 0, 1] = u[TM_PROJ - SHORT_PAD:]
    zeros = jnp.zeros((SHORT_PAD, LANES), f32)
    slabs = []
    for s in range(N_SLAB):
        ls = slice(s * LANES, (s + 1) * LANES)
        uwin_ref[s, 0:SHORT_PAD, :] = zeros
        uwin_ref[s, SHORT_PAD:SHORT_PAD + TM_PROJ, :] = u[:, ls]
        uwin_ref[s, SHORT_PAD + TM_PROJ:, :] = zeros
        acc = jnp.broadcast_to(csb_ref[0, :, ls], (TM_PROJ, LANES))
        for k in range(SHORT_K):
            shift = SHORT_PAD - (SHORT_K - 1) // 2 + k
            acc = acc + csw_ref[0, k:k + 1, ls] * uwin_ref[s, shift:shift + TM_PROJ, :]
        slabs.append(acc)
    ys_ref[0] = jnp.concatenate(slabs, axis=1)


def _depthwise(win_ref, base, r0, rows, w_ref, b_ref):
    off = CONF_PAD - (CONF_K - 1) // 2
    park = SEQ + 2 * CONF_PAD
    accs = []
    for s in range(N_SLAB):
        ls = slice(s * LANES, (s + 1) * LANES)
        acc = jnp.broadcast_to(b_ref[0, :, ls], (rows, LANES))
        for k in range(CONF_K):
            acc = acc + w_ref[0, k:k + 1, ls] * win_ref[s, pl.ds(base + r0 + off + k, rows), :]
        win_ref[s, park:park + rows, :] = acc
        accs.append(win_ref[s, park:park + rows, :])
    return jnp.concatenate(accs, axis=1)


def _seq_phase(j, tab_ref, y_ref, xx_ref):
    k_rows = slice(j * TM, (j + 1) * TM)
    even = _dot(tab_ref[0, k_rows, :], xx_ref[0])
    odd = _dot(tab_ref[1, k_rows, :], xx_ref[1])
    y_ref[0, 0] = even + odd
    y_ref[0, 1] = even - odd


def _short_conv_tile(ys_ref, prev_ref, next_ref, csw_ref):
    j, last = pl.program_id(1), SEQ // TM_MERGE - 1
    row = lax.broadcasted_iota(jnp.int32, (SHORT_PAD, 1), 0)
    before = csw_ref[0, 0:1, :] * prev_ref[0, 0, 1, SHORT_PAD - 1:SHORT_PAD, :]
    after = csw_ref[0, SHORT_K - 1:SHORT_K, :] * next_ref[0, 0, 0, 0:1, :]
    ys = ys_ref[0]
    top = ys[:SHORT_PAD] + jnp.where((row == 0) & (j > 0), before, 0.0)
    bot = ys[TM_MERGE - SHORT_PAD:] + jnp.where((row == SHORT_PAD - 1) & (j < last), after, 0.0)
    return jnp.concatenate([top, ys[SHORT_PAD:TM_MERGE - SHORT_PAD], bot], axis=0)


def _kernel_b(x_ref, h_ref, y_ref, cv_ref, ys_ref, prev_ref, next_ref, csw_ref,
              w_hbm, wbr_hbm, wo_hbm, b_ref, lng_ref, lnb_ref, fg_ref, o_ref,
              wq_ref, wbr_ref, wo_ref, stage_ref, sem_ref, *, layer, final):
    @pl.when(_first_step())
    def _():
        halves = range(0, D_MODEL, BRANCH_W)
        pairs = _w_in_pairs(w_hbm, layer, B_BLOCKS, wq_ref)
        pairs += [(wbr_hbm.at[layer, k, :, pl.ds(c, BRANCH_W)], wbr_ref.at[k, :, pl.ds(c, BRANCH_W)])
                  for k in range(N_BRANCH) for c in halves]
        pairs += [(wo_hbm.at[layer, pl.ds(r, STAGE_ROWS), pl.ds(c, BRANCH_W)],
                   wo_ref.at[pl.ds(r, STAGE_ROWS), pl.ds(c, BRANCH_W)])
                  for r in range(0, D_MODEL, STAGE_ROWS) for c in halves]
        _stage_weights(pairs, stage_ref, sem_ref)

    x = x_ref[0]

    def proj(block, n=1):
        return _project(h_ref[0], wq_ref, b_ref, B_BLOCKS, block, n)

    def gate(k):
        return jax.nn.sigmoid(proj(_GATE0 + 2 * k, 2))

    y_f = y_ref[0] * _silu(proj(_FZ))
    m = gate(0) * _dot(y_f.astype(bf16), wbr_ref[0])

    cv = cv_ref[0]
    cc = cv - jnp.mean(cv, axis=-1, keepdims=True)
    var = jnp.mean(cc * cc, axis=-1, keepdims=True)
    ln = cc * lax.rsqrt(var + EPS) * lng_ref[0] + lnb_ref[0]
    y_c = _silu(ln) * _silu(proj(_CZ))
    m = m + gate(1) * _dot(y_c.astype(bf16), wbr_ref[1])

    y_s = proj(_SB) * _short_conv_tile(ys_ref, prev_ref, next_ref, csw_ref) * _silu(proj(_SZ))
    m = m + gate(2) * _dot(y_s.astype(bf16), wbr_ref[2])

    mb = m.astype(bf16)
    for r in range(0, TM_MERGE, TM_MERGE // 2):
        rows = slice(r, r + TM_MERGE // 2)
        out = x[rows] + _dot(mb[rows], wo_ref[...])
        if final:
            out = out * lax.rsqrt(jnp.mean(out * out, axis=-1, keepdims=True) + EPS) * fg_ref[...]
        o_ref[0, rows, :] = out


def _resident(shape, *index):
    return pl.BlockSpec(shape, lambda b, j: index, pipeline_mode=pl.Buffered(1))


def _layer_row(width, l):
    return _resident((1, 1, width), l, 0, 0)


_HBM = pl.BlockSpec(memory_space=pl.ANY)
def _stage_scratch(slots):
    return [pltpu.VMEM((slots, STAGE_ROWS, BRANCH_W), f32), pltpu.SemaphoreType.DMA((slots,))]


def _params():
    return pltpu.CompilerParams(dimension_semantics=("arbitrary", "arbitrary"),
                                vmem_limit_bytes=VMEM_LIMIT)


def _tile(width, rows=TM):
    return pl.BlockSpec((1, rows, width), lambda b, j: (b, j, 0))


_SLAB_SHAPE = jax.ShapeDtypeStruct((BATCH, N_SLAB, SEQ, LANES), f32)


_EDGE_SHAPE = jax.ShapeDtypeStruct((BATCH, SEQ // TM_PROJ, 2, SHORT_PAD, BRANCH_W), f32)


def _call_af(l, x, norm_g, w_in, b_in, csg, csw, csb, tab, ccw, ccb):
    proj_step = lambda p: jnp.minimum(p, N_PROJ - 1)
    seq_step = lambda p: jnp.maximum(p - N_PROJ, 0)
    halves = pl.BlockSpec((1, 2, TM, BRANCH_W), lambda b, p: (b, 0, seq_step(p), 0))
    half_shape = jax.ShapeDtypeStruct((BATCH, 2, HALF, BRANCH_W), f32)
    return pl.pallas_call(
        functools.partial(_kernel_af, layer=l),
        grid=(BATCH, N_PROJ + NT_HALF),
        in_specs=[pl.BlockSpec((1, TM_PROJ, D_MODEL), lambda b, p: (b, proj_step(p), 0)),
                  _layer_row(D_MODEL, l), _HBM,
                  _layer_row(IN_W, l), _resident((PAIR_W, 2 * PAIR_W), 0, 0),
                  _resident((1, SHORT_K, BRANCH_W), l, 0, 0), _layer_row(BRANCH_W, l),
                  _resident((2, HALF, SEQ), 0, 0, 0),
                  _resident((1, CONF_K, BRANCH_W), l, 0, 0), _layer_row(BRANCH_W, l)],
        out_specs=[pl.BlockSpec((1, TM_PROJ, D_MODEL), lambda b, p: (b, proj_step(p), 0)),
                   pl.BlockSpec((1, TM_PROJ, BRANCH_W), lambda b, p: (b, proj_step(p), 0)),
                   pl.BlockSpec((1, 1, 2, SHORT_PAD, BRANCH_W),
                                lambda b, p: (b, proj_step(p), 0, 0, 0)),
                   halves,
                   pl.BlockSpec((1, SEQ, BRANCH_W), lambda b, p: (b, 0, 0))],
        out_shape=[jax.ShapeDtypeStruct((BATCH, SEQ, D_MODEL), bf16),
                   jax.ShapeDtypeStruct((BATCH, SEQ, BRANCH_W), f32),
                   _EDGE_SHAPE, half_shape,
                   jax.ShapeDtypeStruct((BATCH, SEQ, BRANCH_W), f32)],
        scratch_shapes=[pltpu.VMEM((2 * PAIR_W // LANES, TM_PROJ, LANES), f32),
                        pltpu.VMEM((N_SLAB, TM_PROJ + 2 * SHORT_PAD, LANES), f32),
                        pltpu.VMEM((D_MODEL, len(A_BLOCKS) * BRANCH_W), bf16)]
                       + _stage_scratch(2)
                       + [pltpu.VMEM((2, SEQ, BRANCH_W), bf16),
                          pltpu.VMEM((N_SLAB, SEQ + 2 * CONF_PAD + CONV_ROWS, LANES), f32)],
        compiler_params=_params(),
        name="proj_seq_mix",
    )(x, norm_g, w_in, b_in, csg, csw, csb, tab, ccw, ccb)


def _call_b(l, x, h, y, cv, ys, edges, csw, w_in, b_in, lng, lnb, wbr, wo, fg):
    assert TM_MERGE == TM_PROJ
    last = SEQ // TM_MERGE - 1
    edge_shape = (1, 1, 2, SHORT_PAD, BRANCH_W)
    return pl.pallas_call(
        functools.partial(_kernel_b, layer=l, final=(l == DEPTH - 1)),
        grid=(BATCH, SEQ // TM_MERGE),
        in_specs=[_tile(D_MODEL, TM_MERGE), _tile(D_MODEL, TM_MERGE)]
                 + [_tile(BRANCH_W, TM_MERGE)] * 3
                 + [pl.BlockSpec(edge_shape, lambda b, j: (b, jnp.maximum(j - 1, 0), 0, 0, 0)),
                    pl.BlockSpec(edge_shape, lambda b, j: (b, jnp.minimum(j + 1, last), 0, 0, 0)),
                    _resident((1, SHORT_K, BRANCH_W), l, 0, 0),
                    _HBM, _HBM, _HBM,
                    _layer_row(IN_W, l), _layer_row(BRANCH_W, l), _layer_row(BRANCH_W, l),
                    _resident((1, D_MODEL), 0, 0)],
        out_specs=_tile(D_MODEL, TM_MERGE),
        out_shape=jax.ShapeDtypeStruct((BATCH, SEQ, D_MODEL), f32),
        scratch_shapes=[pltpu.VMEM((D_MODEL, len(B_BLOCKS) * BRANCH_W), bf16),
                        pltpu.VMEM((N_BRANCH, BRANCH_W, D_MODEL), bf16),
                        pltpu.VMEM((D_MODEL, D_MODEL), bf16)] + _stage_scratch(STAGE_SLOTS),
        compiler_params=_params(),
        name="merge_out",
    )(x, h, y, cv, ys, edges, edges, csw, w_in, wbr, wo, b_in, lng, lnb, fg)


def kernel(x, norm_g, w_in, b_in, conv_c_w, conv_c_b, ln_c_g, ln_c_b,
           conv_s_w, conv_s_b, w_branch, w_out, final_g):
    tab = jnp.asarray(_SEQ_TABLE).astype(bf16)
    csg = jnp.asarray(_CSG_TABLE).astype(bf16)
    rows = lambda v: v.reshape(DEPTH, 1, -1)
    norm_g, b_in, conv_c_b, ln_c_g, ln_c_b, conv_s_b = map(
        rows, (norm_g, b_in, conv_c_b, ln_c_g, ln_c_b, conv_s_b))
    final_g = final_g.reshape(1, -1)
    for l in range(DEPTH):
        h, ys, edges, y, cv = _call_af(l, x, norm_g, w_in, b_in, csg, conv_s_w, conv_s_b,
                                       tab, conv_c_w, conv_c_b)
        y = y.reshape(BATCH, SEQ, BRANCH_W)
        x = _call_b(l, x, h, y, cv, ys, edges, conv_s_w, w_in, b_in, ln_c_g, ln_c_b,
                    w_branch, w_out, final_g)
    return x
```

```python
import functools

import numpy as np
import jax
import jax.numpy as jnp
from jax import lax
from jax.experimental import pallas as pl
from jax.experimental.pallas import tpu as pltpu

D_MODEL = 1024
BATCH = 8
SEQ = 2048
DEPTH = 2
BRANCH_W = D_MODEL // 2
N_BRANCH = 3
N_GROUPS = 4
GROUP_W = BRANCH_W // N_GROUPS
PAIR_W = 2 * GROUP_W
CONF_K = 31
SHORT_K = 3
EPS = 1e-6
IN_W = 9 * BRANCH_W + N_BRANCH * D_MODEL

TM = 512
NT = SEQ // TM
TM_PROJ = 1024
N_PROJ = SEQ // TM_PROJ
TM_MERGE = 1024
LANES = 128
N_SLAB = BRANCH_W // LANES
HALF = SEQ // 2
NT_HALF = NT // 2
CONV_ROWS = 40
STAGE_ROWS = BRANCH_W
STAGE_SLOTS = 4
CONV_SPLITS = (1000, 1520, SEQ)
CONF_PAD = 16
SHORT_PAD = 8
VMEM_LIMIT = 63 * 1024 * 1024

_FX, _FZ, _CA, _CB, _CZ, _SB, _SC, _SH, _SZ = range(9)
_GATE0 = 9
A_BLOCKS = (_FX, _CA, _CB, _SC, _SH)
B_BLOCKS = (_FZ, _CZ, _SB, _SZ) + tuple(range(_GATE0, _GATE0 + 2 * N_BRANCH))

bf16 = jnp.bfloat16
f32 = jnp.float32


def _dft_tables():
    k = np.arange(HALF, dtype=np.int64)[:, None]
    m = np.arange(HALF, dtype=np.int64)[None, :]
    seq = []
    for parity in range(2):
        ang = 2.0 * np.pi * (((2 * m + parity) * k) % SEQ).astype(np.float64) / SEQ
        seq.append(np.concatenate([np.cos(ang), -np.sin(ang)], axis=1))
    seq = np.stack(seq)
    c = np.arange(GROUP_W, dtype=np.int64)
    angg = 2.0 * np.pi * ((c[:, None] * c[None, :]) % GROUP_W).astype(np.float64) / GROUP_W
    scale = 1.0 / np.sqrt(float(SEQ * GROUP_W))
    pair = np.eye(PAIR_W // GROUP_W)
    cg = np.kron(pair, np.cos(angg)) * scale
    sg = np.kron(pair, np.sin(angg)) * scale
    return seq.astype(np.float32), np.concatenate([cg, sg], axis=1).astype(np.float32)


_SEQ_TABLE, _CSG_TABLE = _dft_tables()


def _dot(a, b):
    return jnp.dot(a, b, preferred_element_type=f32)


def _silu(v):
    return v * jax.nn.sigmoid(v)


def _rms_bf16(x, g):
    y = x * lax.rsqrt(jnp.mean(x * x, axis=-1, keepdims=True) + EPS)
    return (y * g).astype(bf16)


def _first_step():
    return (pl.program_id(0) == 0) & (pl.program_id(1) == 0)


def _stage_weights(pairs, stage_ref, sem_ref):
    slots = stage_ref.shape[0]

    def copy(i):
        return pltpu.make_async_copy(pairs[i][0], stage_ref.at[i % slots], sem_ref.at[i % slots])

    for i in range(min(slots - 1, len(pairs))):
        copy(i).start(priority=i % 2)
    for i, (_, dst) in enumerate(pairs):
        if i + slots - 1 < len(pairs):
            copy(i + slots - 1).start(priority=(i + slots - 1) % 2)
        copy(i).wait()
        dst[...] = stage_ref[i % slots].astype(bf16)


def _w_in_pairs(w_hbm, l, blocks, wq_ref):
    return [(w_hbm.at[l, pl.ds(r, STAGE_ROWS), pl.ds(blk * BRANCH_W, BRANCH_W)],
             wq_ref.at[pl.ds(r, STAGE_ROWS), pl.ds(i * BRANCH_W, BRANCH_W)])
            for i, blk in enumerate(blocks) for r in range(0, D_MODEL, STAGE_ROWS)]


def _project(h, wq_ref, b_ref, blocks, block, n=1):
    i = blocks.index(block)
    assert blocks[i:i + n] == tuple(range(block, block + n))
    return (_dot(h, wq_ref[:, i * BRANCH_W:(i + n) * BRANCH_W])
            + b_ref[0, :, block * BRANCH_W:(block + n) * BRANCH_W])


def _kernel_af(x_ref, g_ref, w_hbm, b_ref, csg_ref, csw_ref, csb_ref, tab_ref, ccw_ref, ccb_ref,
               h_ref, ys_ref, edge_ref, y_ref, cv_ref,
               split_ref, uwin_ref, wq_ref, stage_ref, sem_ref, xx_ref, cwin_ref,
               *, layer):
    assert N_PROJ == 2 and NT_HALF == 2
    p = pl.program_id(1)
    conv = functools.partial(_conv_rows, cwin_ref, ccw_ref, ccb_ref, cv_ref)

    @pl.when(_first_step())
    def _():
        _stage_weights(_w_in_pairs(w_hbm, layer, A_BLOCKS, wq_ref), stage_ref, sem_ref)

    for step in range(N_PROJ):
        @pl.when(p == step)
        def _(step=step):
            if step == N_PROJ - 1:
                conv(p - step, 0, CONV_SPLITS[0])
            _proj_phase(step, x_ref, g_ref, b_ref, csg_ref, csw_ref, csb_ref, h_ref, ys_ref,
                        edge_ref, split_ref, uwin_ref, wq_ref, xx_ref, cwin_ref)

    for j in range(NT_HALF):
        @pl.when(p == N_PROJ + j)
        def _(j=j):
            conv(p - N_PROJ - j, CONV_SPLITS[j], CONV_SPLITS[j + 1])
            _seq_phase(j, tab_ref, y_ref, xx_ref)


def _conv_rows(win_ref, w_ref, b_ref, cv_ref, zero, lo, hi):
    for r0 in range(lo, hi, CONV_ROWS):
        rows = min(CONV_ROWS, hi - r0)
        cv_ref[0, r0:r0 + rows, :] = _depthwise(win_ref, zero, r0, rows, w_ref, b_ref)


def _proj_phase(p, x_ref, g_ref, b_ref, csg_ref, csw_ref, csb_ref, h_ref, ys_ref, edge_ref,
                split_ref, uwin_ref, wq_ref, xx_ref, cwin_ref):
    h = _rms_bf16(x_ref[0], g_ref[0])
    h_ref[0] = h

    proj = functools.partial(_project, h, wq_ref, b_ref, A_BLOCKS)

    fx = proj(_FX).astype(bf16)
    slabs_per_plane = PAIR_W // LANES
    for pair in range(BRANCH_W // PAIR_W):
        cols = slice(pair * PAIR_W, (pair + 1) * PAIR_W)
        xx = _dot(fx[:, cols], csg_ref[...])
        for q in range(2 * slabs_per_plane):
            split_ref[q] = xx[:, q * LANES:(q + 1) * LANES]
        for parity in range(2):
            rows = [split_ref[q, pl.ds(parity, TM_PROJ // 2, stride=2), :]
                    for q in range(2 * slabs_per_plane)]
            for plane in range(2):
                part = rows[plane * slabs_per_plane:(plane + 1) * slabs_per_plane]
                at = plane * HALF + p * (TM_PROJ // 2)
                xx_ref[parity, at:at + TM_PROJ // 2, cols] = (
                    jnp.concatenate(part, axis=1).astype(bf16))

    glu = proj(_CA) * jax.nn.sigmoid(proj(_CB))
    at = CONF_PAD + p * TM_PROJ
    for s in range(N_SLAB):
        cwin_ref[s, 0:CONF_PAD, :] = jnp.zeros((CONF_PAD, LANES), f32)
        cwin_ref[s, CONF_PAD + SEQ:2 * CONF_PAD + SEQ, :] = jnp.zeros((CONF_PAD, LANES), f32)
        cwin_ref[s, at:at + TM_PROJ, :] = glu[:, s * LANES:(s + 1) * LANES]

    u = proj(_SC) * proj(_SH)
    edge_ref[0, 0, 0] = u[:SHORT_PAD]
    edge_ref[0, 0, 1] = u[TM_PROJ - SHORT_PAD:]
    zeros = jnp.zeros((SHORT_PAD, LANES), f32)
    slabs = []
    for s in range(N_SLAB):
        ls = slice(s * LANES, (s + 1) * LANES)
        uwin_ref[s, 0:SHORT_PAD, :] = zeros
        uwin_ref[s, SHORT_PAD:SHORT_PAD + TM_PROJ, :] = u[:, ls]
        uwin_ref[s, SHORT_PAD + TM_PROJ:, :] = zeros
        acc = jnp.broadcast_to(csb_ref[0, :, ls], (TM_PROJ, LANES))
        for k in range(SHORT_K):
            shift = SHORT_PAD - (SHORT_K - 1) // 2 + k
            acc = acc + csw_ref[0, k:k + 1, ls] * uwin_ref[s, shift:shift + TM_PROJ, :]
        slabs.append(acc)
    ys_ref[0] = jnp.concatenate(slabs, axis=1)


def _depthwise(win_ref, base, r0, rows, w_ref, b_ref):
    off = CONF_PAD - (CONF_K - 1) // 2
    park = SEQ + 2 * CONF_PAD
    accs = []
    for s in range(N_SLAB):
        ls = slice(s * LANES, (s + 1) * LANES)
        acc = jnp.broadcast_to(b_ref[0, :, ls], (rows, LANES))
        for k in range(CONF_K):
            acc = acc + w_ref[0, k:k + 1, ls] * win_ref[s, pl.ds(base + r0 + off + k, rows), :]
        win_ref[s, park:park + rows, :] = acc
        accs.append(win_ref[s, park:park + rows, :])
    return jnp.concatenate(accs, axis=1)


def _seq_phase(j, tab_ref, y_ref, xx_ref):
    k_rows = slice(j * TM, (j + 1) * TM)
    even = _dot(tab_ref[0, k_rows, :], xx_ref[0])
    odd = _dot(tab_ref[1, k_rows, :], xx_ref[1])
    y_ref[0, 0] = even + odd
    y_ref[0, 1] = even - odd


def _short_conv_tile(ys_ref, prev_ref, next_ref, csw_ref):
    j, last = pl.program_id(1), SEQ // TM_MERGE - 1
    row = lax.broadcasted_iota(jnp.int32, (SHORT_PAD, 1), 0)
    before = csw_ref[0, 0:1, :] * prev_ref[0, 0, 1, SHORT_PAD - 1:SHORT_PAD, :]
    after = csw_ref[0, SHORT_K - 1:SHORT_K, :] * next_ref[0, 0, 0, 0:1, :]
    ys = ys_ref[0]
    top = ys[:SHORT_PAD] + jnp.where((row == 0) & (j > 0), before, 0.0)
    bot = ys[TM_MERGE - SHORT_PAD:] + jnp.where((row == SHORT_PAD - 1) & (j < last), after, 0.0)
    return jnp.concatenate([top, ys[SHORT_PAD:TM_MERGE - SHORT_PAD], bot], axis=0)


def _kernel_b(x_ref, h_ref, y_ref, cv_ref, ys_ref, prev_ref, next_ref, csw_ref,
              w_hbm, wbr_hbm, wo_hbm, b_ref, lng_ref, lnb_ref, fg_ref, o_ref,
              wq_ref, wbr_ref, wo_ref, stage_ref, sem_ref, *, layer, final):
    @pl.when(_first_step())
    def _():
        halves = range(0, D_MODEL, BRANCH_W)
        pairs = _w_in_pairs(w_hbm, layer, B_BLOCKS, wq_ref)
        pairs += [(wbr_hbm.at[layer, k, :, pl.ds(c, BRANCH_W)], wbr_ref.at[k, :, pl.ds(c, BRANCH_W)])
                  for k in range(N_BRANCH) for c in halves]
        pairs += [(wo_hbm.at[layer, pl.ds(r, STAGE_ROWS), pl.ds(c, BRANCH_W)],
                   wo_ref.at[pl.ds(r, STAGE_ROWS), pl.ds(c, BRANCH_W)])
                  for r in range(0, D_MODEL, STAGE_ROWS) for c in halves]
        _stage_weights(pairs, stage_ref, sem_ref)

    x = x_ref[0]

    def proj(block, n=1):
        return _project(h_ref[0], wq_ref, b_ref, B_BLOCKS, block, n)

    def gate(k):
        return jax.nn.sigmoid(proj(_GATE0 + 2 * k, 2))

    y_f = y_ref[0] * _silu(proj(_FZ))
    m = gate(0) * _dot(y_f.astype(bf16), wbr_ref[0])

    cv = cv_ref[0]
    cc = cv - jnp.mean(cv, axis=-1, keepdims=True)
    var = jnp.mean(cc * cc, axis=-1, keepdims=True)
    ln = cc * lax.rsqrt(var + EPS) * lng_ref[0] + lnb_ref[0]
    y_c = _silu(ln) * _silu(proj(_CZ))
    m = m + gate(1) * _dot(y_c.astype(bf16), wbr_ref[1])

    y_s = proj(_SB) * _short_conv_tile(ys_ref, prev_ref, next_ref, csw_ref) * _silu(proj(_SZ))
    m = m + gate(2) * _dot(y_s.astype(bf16), wbr_ref[2])

    mb = m.astype(bf16)
    for r in range(0, TM_MERGE, TM_MERGE // 2):
        rows = slice(r, r + TM_MERGE // 2)
        out = x[rows] + _dot(mb[rows], wo_ref[...])
        if final:
            out = out * lax.rsqrt(jnp.mean(out * out, axis=-1, keepdims=True) + EPS) * fg_ref[...]
        o_ref[0, rows, :] = out


def _resident(shape, *index):
    return pl.BlockSpec(shape, lambda b, j: index, pipeline_mode=pl.Buffered(1))


def _layer_row(width, l):
    return _resident((1, 1, width), l, 0, 0)


_HBM = pl.BlockSpec(memory_space=pl.ANY)
def _stage_scratch(slots):
    return [pltpu.VMEM((slots, STAGE_ROWS, BRANCH_W), f32), pltpu.SemaphoreType.DMA((slots,))]


def _params():
    return pltpu.CompilerParams(dimension_semantics=("arbitrary", "arbitrary"),
                                vmem_limit_bytes=VMEM_LIMIT)


def _tile(width, rows=TM):
    return pl.BlockSpec((1, rows, width), lambda b, j: (b, j, 0))


_SLAB_SHAPE = jax.ShapeDtypeStruct((BATCH, N_SLAB, SEQ, LANES), f32)


_EDGE_SHAPE = jax.ShapeDtypeStruct((BATCH, SEQ // TM_PROJ, 2, SHORT_PAD, BRANCH_W), f32)


def _call_af(l, x, norm_g, w_in, b_in, csg, csw, csb, tab, ccw, ccb):
    proj_step = lambda p: jnp.minimum(p, N_PROJ - 1)
    seq_step = lambda p: jnp.maximum(p - N_PROJ, 0)
    halves = pl.BlockSpec((1, 2, TM, BRANCH_W), lambda b, p: (b, 0, seq_step(p), 0))
    half_shape = jax.ShapeDtypeStruct((BATCH, 2, HALF, BRANCH_W), f32)
    return pl.pallas_call(
        functools.partial(_kernel_af, layer=l),
        grid=(BATCH, N_PROJ + NT_HALF),
        in_specs=[pl.BlockSpec((1, TM_PROJ, D_MODEL), lambda b, p: (b, proj_step(p), 0)),
                  _layer_row(D_MODEL, l), _HBM,
                  _layer_row(IN_W, l), _resident((PAIR_W, 2 * PAIR_W), 0, 0),
                  _resident((1, SHORT_K, BRANCH_W), l, 0, 0), _layer_row(BRANCH_W, l),
                  _resident((2, HALF, SEQ), 0, 0, 0),
                  _resident((1, CONF_K, BRANCH_W), l, 0, 0), _layer_row(BRANCH_W, l)],
        out_specs=[pl.BlockSpec((1, TM_PROJ, D_MODEL), lambda b, p: (b, proj_step(p), 0)),
                   pl.BlockSpec((1, TM_PROJ, BRANCH_W), lambda b, p: (b, proj_step(p), 0)),
                   pl.BlockSpec((1, 1, 2, SHORT_PAD, BRANCH_W),
                                lambda b, p: (b, proj_step(p), 0, 0, 0)),
                   halves,
                   pl.BlockSpec((1, SEQ, BRANCH_W), lambda b, p: (b, 0, 0))],
        out_shape=[jax.ShapeDtypeStruct((BATCH, SEQ, D_MODEL), bf16),
                   jax.ShapeDtypeStruct((BATCH, SEQ, BRANCH_W), f32),
                   _EDGE_SHAPE, half_shape,
                   jax.ShapeDtypeStruct((BATCH, SEQ, BRANCH_W), f32)],
        scratch_shapes=[pltpu.VMEM((2 * PAIR_W // LANES, TM_PROJ, LANES), f32),
                        pltpu.VMEM((N_SLAB, TM_PROJ + 2 * SHORT_PAD, LANES), f32),
                        pltpu.VMEM((D_MODEL, len(A_BLOCKS) * BRANCH_W), bf16)]
                       + _stage_scratch(2)
                       + [pltpu.VMEM((2, SEQ, BRANCH_W), bf16),
                          pltpu.VMEM((N_SLAB, SEQ + 2 * CONF_PAD + CONV_ROWS, LANES), f32)],
        compiler_params=_params(),
        name="proj_seq_mix",
    )(x, norm_g, w_in, b_in, csg, csw, csb, tab, ccw, ccb)


def _call_b(l, x, h, y, cv, ys, edges, csw, w_in, b_in, lng, lnb, wbr, wo, fg):
    assert TM_MERGE == TM_PROJ
    last = SEQ // TM_MERGE - 1
    edge_shape = (1, 1, 2, SHORT_PAD, BRANCH_W)
    return pl.pallas_call(
        functools.partial(_kernel_b, layer=l, final=(l == DEPTH - 1)),
        grid=(BATCH, SEQ // TM_MERGE),
        in_specs=[_tile(D_MODEL, TM_MERGE), _tile(D_MODEL, TM_MERGE)]
                 + [_tile(BRANCH_W, TM_MERGE)] * 3
                 + [pl.BlockSpec(edge_shape, lambda b, j: (b, jnp.maximum(j - 1, 0), 0, 0, 0)),
                    pl.BlockSpec(edge_shape, lambda b, j: (b, jnp.minimum(j + 1, last), 0, 0, 0)),
                    _resident((1, SHORT_K, BRANCH_W), l, 0, 0),
                    _HBM, _HBM, _HBM,
                    _layer_row(IN_W, l), _layer_row(BRANCH_W, l), _layer_row(BRANCH_W, l),
                    _resident((1, D_MODEL), 0, 0)],
        out_specs=_tile(D_MODEL, TM_MERGE),
        out_shape=jax.ShapeDtypeStruct((BATCH, SEQ, D_MODEL), f32),
        scratch_shapes=[pltpu.VMEM((D_MODEL, len(B_BLOCKS) * BRANCH_W), bf16),
                        pltpu.VMEM((N_BRANCH, BRANCH_W, D_MODEL), bf16),
                        pltpu.VMEM((D_MODEL, D_MODEL), bf16)] + _stage_scratch(STAGE_SLOTS),
        compiler_params=_params(),
        name="merge_out",
    )(x, h, y, cv, ys, edges, edges, csw, w_in, wbr, wo, b_in, lng, lnb, fg)


def kernel(x, norm_g, w_in, b_in, conv_c_w, conv_c_b, ln_c_g, ln_c_b,
           conv_s_w, conv_s_b, w_branch, w_out, final_g):
    tab = jnp.asarray(_SEQ_TABLE).astype(bf16)
    csg = jnp.asarray(_CSG_TABLE).astype(bf16)
    rows = lambda v: v.reshape(DEPTH, 1, -1)
    norm_g, b_in, conv_c_b, ln_c_g, ln_c_b, conv_s_b = map(
        rows, (norm_g, b_in, conv_c_b, ln_c_g, ln_c_b, conv_s_b))
    final_g = final_g.reshape(1, -1)
    for l in range(DEPTH):
        h, ys, edges, y, cv = _call_af(l, x, norm_g, w_in, b_in, csg, conv_s_w, conv_s_b,
                                       tab, conv_c_w, conv_c_b)
        y = y.reshape(BATCH, SEQ, BRANCH_W)
        x = _call_b(l, x, h, y, cv, ys, edges, conv_s_w, w_in, b_in, ln_c_g, ln_c_b,
                    w_branch, w_out, final_g)
    return x
```
